```python
import math
import jax, jax.numpy as jnp
from jax import lax
import numpy as np

D_MODEL = 1024
BATCH = 4
SEQ = 4096
DEPTH = 4
DEC_BATCH = 32
DEC_SEQ = 4
PAST_LEN = 8192
PAGE_SIZE = 128

GLA_HEADS = 4
GLA_DK = D_MODEL // 2
GLA_DV = D_MODEL
GLA_HK = GLA_DK // GLA_HEADS
GLA_HV = GLA_DV // GLA_HEADS
GLA_RANK = 16
GLA_TAU = 16.0
GLA_CHUNK = 64
DIFF_HEADS = 8
DIFF_HD = D_MODEL // (2 * DIFF_HEADS)
DIFF_QK = 2 * DIFF_HEADS * DIFF_HD
DIFF_V = 2 * DIFF_HEADS * DIFF_HD
DIFF_QBLOCK = 128
ROPE_THETA = 10000.0
FFN_HIDDEN = ((8 * D_MODEL // 3 + 127) // 128) * 128
ALPHA = (2.0 * DEPTH) ** 0.25
BETA = (8.0 * DEPTH) ** -0.25
LN_EPS = 1e-5
N_MOD = 9
IN_WIDTH = 2 * GLA_DK + 2 * GLA_DV + GLA_RANK + 2 * DIFF_QK + DIFF_V + 2 * D_MODEL
V_COLUMN_GROUPS = (2, 7)

kernel_name = "hybrid_gla_diffattn_macaron_deepnorm_step"


def _in_widths():
    return (GLA_DK, GLA_DK, GLA_DV, GLA_DV, GLA_RANK, DIFF_QK, DIFF_QK, DIFF_V, D_MODEL, D_MODEL)


def _in_offsets():
    return [int(o) for o in np.cumsum(_in_widths())[:-1]]


def layer_norm(x, g, b):
    xf = x.astype(jnp.float32)
    mu = jnp.mean(xf, -1, keepdims=True)
    var = jnp.mean(jnp.square(xf - mu), -1, keepdims=True)
    return ((xf - mu) * lax.rsqrt(var + LN_EPS) * g + b).astype(x.dtype)


def rms_norm(x, g):
    xf = x.astype(jnp.float32)
    return (xf * lax.rsqrt(jnp.mean(xf * xf, -1, keepdims=True) + LN_EPS) * g).astype(x.dtype)


def rope(x, pos):
    d = x.shape[-1]
    half = d // 2
    inv = ROPE_THETA ** (-jnp.arange(half, dtype=jnp.float32) * (2.0 / d))
    ang = pos.astype(jnp.float32)[:, None] * inv[None, :]
    bshape = (1, pos.shape[0]) + (1,) * (x.ndim - 3) + (half,)
    cos = jnp.cos(ang).reshape(bshape)
    sin = jnp.sin(ang).reshape(bshape)
    xf = x.astype(jnp.float32)
    x1, x2 = xf[..., :half], xf[..., half:]
    return jnp.concatenate([x1 * cos - x2 * sin, x2 * cos + x1 * sin], -1).astype(x.dtype)


def swiglu(u, w_in, w_out):
    a, b = jnp.split(u @ w_in, 2, axis=-1)
    return (jax.nn.silu(a) * b) @ w_out


def gla_recurrence(q, k, v, g, s0):
    B, L, H, _ = q.shape
    C = min(GLA_CHUNK, L)
    n = -(-L // C)
    pad = n * C - L
    if pad:
        pw = ((0, 0), (0, pad), (0, 0), (0, 0))
        q, k, v, g = jnp.pad(q, pw), jnp.pad(k, pw), jnp.pad(v, pw), jnp.pad(g, pw)

    def to_chunks(t):
        return t.reshape(B, n, C, H, t.shape[-1]).transpose(1, 0, 3, 2, 4).astype(jnp.float32)

    causal = jnp.tril(jnp.ones((C, C), dtype=bool))[:, :, None]

    def step(S, inp):
        qc, kc, vc, gc = inp
        cum = jnp.cumsum(gc, axis=2)
        o_inter = jnp.einsum('bhtk,bhkv->bhtv', qc * jnp.exp(cum), S)
        diff = cum[:, :, :, None, :] - cum[:, :, None, :, :]
        decay = jnp.exp(jnp.where(causal, diff, -jnp.inf))
        att = jnp.einsum('bhtk,bhsk,bhtsk->bhts', qc, kc, decay)
        o_intra = jnp.einsum('bhts,bhsv->bhtv', att, vc)
        last = cum[:, :, -1:, :]
        S_new = jnp.exp(last[:, :, 0, :])[..., None] * S + jnp.einsum(
            'bhsk,bhsv->bhkv', kc * jnp.exp(last - cum), vc)
        return S_new, o_inter + o_intra

    S_fin, o = lax.scan(step, s0.astype(jnp.float32),
                        (to_chunks(q), to_chunks(k), to_chunks(v), to_chunks(g)))
    o = o.transpose(1, 0, 3, 2, 4).reshape(B, n * C, H, v.shape[-1])[:, :L]
    return o.astype(v.dtype), S_fin.astype(s0.dtype)


def diff_core(q, k, v, q_pos, k_pos, lam, lam_init, norm_g):
    s = jnp.einsum('bqhcd,bkhcd->bhcqk', q.astype(jnp.float32), k.astype(jnp.float32)) * (DIFF_HD ** -0.5)
    mask = k_pos[None, :] <= q_pos[:, None]
    p = jax.nn.softmax(jnp.where(mask, s, -jnp.inf), axis=-1)
    p = p[:, :, 0] - lam * p[:, :, 1]
    o = jnp.einsum('bhqk,bkhe->bqhe', p, v.astype(jnp.float32))
    return (rms_norm(o, norm_g) * (1.0 - lam_init)).astype(v.dtype)


def diff_attn_blocks(q, k, v, pos, lam, lam_init, norm_g):
    B, L = q.shape[:2]
    QB = min(DIFF_QBLOCK, L)
    nb = L // QB
    qb = q.reshape(B, nb, QB, DIFF_HEADS, 2, DIFF_HD).transpose(1, 0, 2, 3, 4, 5)
    pb = pos.reshape(nb, QB)
    o = lax.map(lambda a: diff_core(a[0], k, v, a[1], pos, lam, lam_init, norm_g), (qb, pb))
    return o.transpose(1, 0, 2, 3, 4).reshape(B, L, DIFF_HEADS, 2 * DIFF_HD)


def decoder_layer(x, c, pos, l, s0, k_past, v_past,
                  w_ada, b_ada, ln_g, ln_b, ffn_w_in, ffn_w_out, w_in,
                  gla_w_f2, gla_b_f2, gla_norm_g, diff_lam, diff_norm_g,
                  w_branch_a, w_branch_b, w_out):
    B, L, _ = x.shape
    mod = (jax.nn.silu(c) @ w_ada + b_ada).reshape(B, N_MOD, 1, D_MODEL)

    def modulate(h, i):
        return h * (1.0 + mod[:, 3 * i + 1]) + mod[:, 3 * i]

    def post(h, i, delta):
        return layer_norm(ALPHA * h + mod[:, 3 * i + 2] * delta, ln_g[i], ln_b[i])

    x = post(x, 0, 0.5 * swiglu(modulate(x, 0), ffn_w_in[0], ffn_w_out[0]))

    u = modulate(x, 1)
    gq, gk, gv, gr, gf, dq, dk, dv, ga, gb = jnp.split(u @ w_in, _in_offsets(), axis=-1)

    glog = jax.nn.log_sigmoid((gf @ gla_w_f2 + gla_b_f2).astype(jnp.float32)) / GLA_TAU
    def gheads(t, d):
        return t.reshape(B, L, GLA_HEADS, d)
    o_a, s_new = gla_recurrence(gheads(gq, GLA_HK) * (GLA_HK ** -0.5), gheads(gk, GLA_HK),
                                gheads(gv, GLA_HV), gheads(glog, GLA_HK), s0)
    o_a = rms_norm(o_a, gla_norm_g) * jax.nn.silu(gheads(gr, GLA_HV))
    y_a = o_a.reshape(B, L, GLA_DV) @ w_branch_a

    lam_init = 0.8 - 0.6 * math.exp(-0.3 * l)
    lf = diff_lam.astype(jnp.float32)
    lam = jnp.exp(jnp.sum(lf[0] * lf[1])) - jnp.exp(jnp.sum(lf[2] * lf[3])) + lam_init
    q = rope(dq.reshape(B, L, DIFF_HEADS, 2, DIFF_HD), pos)
    k = rope(dk.reshape(B, L, DIFF_HEADS, 2, DIFF_HD), pos)
    v = dv.reshape(B, L, DIFF_HEADS, 2 * DIFF_HD)
    if k_past is None:
        o_b = diff_attn_blocks(q, k, v, pos, lam, lam_init, diff_norm_g)
    else:
        P = k_past.shape[1]
        k_all = jnp.concatenate([k_past.astype(k.dtype), k], axis=1)
        v_all = jnp.concatenate([v_past.astype(v.dtype), v], axis=1)
        k_pos = jnp.arange(P + L, dtype=jnp.int32)
        o_b = diff_core(q, k_all, v_all, pos, k_pos, lam, lam_init, diff_norm_g)
    y_b = o_b.reshape(B, L, DIFF_V) @ w_branch_b

    merged = jax.nn.sigmoid(ga) * y_a + jax.nn.sigmoid(gb) * y_b
    x = post(x, 1, merged @ w_out)

    x = post(x, 2, 0.5 * swiglu(modulate(x, 2), ffn_w_in[1], ffn_w_out[1]))
    return x, s_new, k, v


def setup_inputs(seed: int = 0) -> dict:
    key = jax.random.key(seed)
    ks = jax.random.split(key, 24)
    f32 = jnp.float32
    n_pages = PAST_LEN // PAGE_SIZE
    n_pool = (DEC_BATCH * n_pages * 5) // 4

    def nrm(k, shape, scale):
        return jax.random.normal(k, shape, f32) * scale

    col_scale = np.concatenate([np.full((w,), BETA if i in V_COLUMN_GROUPS else 1.0, np.float32)
                                for i, w in enumerate(_in_widths())])
    page_table = jax.random.permutation(ks[5], n_pool)[: DEC_BATCH * n_pages]
    page_table = page_table.reshape(DEC_BATCH, n_pages).astype(jnp.int32)
    return {
        "x_prompt": nrm(ks[0], (BATCH, SEQ, D_MODEL), 1.0),
        "x_sample": nrm(ks[1], (DEC_BATCH, DEC_SEQ, D_MODEL), 1.0),
        "cache_k": nrm(ks[2], (DEPTH, n_pool, PAGE_SIZE, DIFF_HEADS, 2, DIFF_HD), 1.0),
        "cache_v": nrm(ks[3], (DEPTH, n_pool, PAGE_SIZE, DIFF_HEADS, 2 * DIFF_HD), BETA),
        "state_gla": nrm(ks[4], (DEPTH, DEC_BATCH, GLA_HEADS, GLA_HK, GLA_HV), 0.5),
        "page_table": page_table,
        "c_prompt": nrm(ks[6], (BATCH, D_MODEL), 1.0),
        "c_sample": nrm(ks[7], (DEC_BATCH, D_MODEL), 1.0),
        "w_ada": nrm(ks[8], (DEPTH, D_MODEL, N_MOD * D_MODEL), D_MODEL ** -0.5),
        "b_ada": nrm(ks[9], (DEPTH, N_MOD * D_MODEL), 0.02),
        "ln_g": 1.0 + nrm(ks[10], (DEPTH, 3, D_MODEL), 0.02),
        "ln_b": nrm(ks[11], (DEPTH, 3, D_MODEL), 0.02),
        "ffn_w_in": nrm(ks[12], (DEPTH, 2, D_MODEL, 2 * FFN_HIDDEN), BETA * D_MODEL ** -0.5),
        "ffn_w_out": nrm(ks[13], (DEPTH, 2, FFN_HIDDEN, D_MODEL), BETA * FFN_HIDDEN ** -0.5),
        "w_in": nrm(ks[14], (DEPTH, D_MODEL, IN_WIDTH), D_MODEL ** -0.5) * jnp.asarray(col_scale),
        "gla_w_f2": nrm(ks[15], (DEPTH, GLA_RANK, GLA_DK), GLA_RANK ** -0.5),
        "gla_b_f2": nrm(ks[16], (DEPTH, GLA_DK), 0.1),
        "gla_norm_g": 1.0 + nrm(ks[17], (DEPTH, GLA_HV), 0.02),
        "diff_lam": nrm(ks[18], (DEPTH, 4, DIFF_HD), 0.1),
        "diff_norm_g": 1.0 + nrm(ks[19], (DEPTH, 2 * DIFF_HD), 0.02),
        "w_branch_a": nrm(ks[20], (DEPTH, GLA_DV, D_MODEL), GLA_DV ** -0.5),
        "w_branch_b": nrm(ks[21], (DEPTH, DIFF_V, D_MODEL), DIFF_V ** -0.5),
        "w_out": nrm(ks[22], (DEPTH, D_MODEL, D_MODEL), BETA * D_MODEL ** -0.5),
    }


def reference(x_prompt, x_sample, cache_k, cache_v, state_gla, page_table, c_prompt, c_sample,
              w_ada, b_ada, ln_g, ln_b, ffn_w_in, ffn_w_out, w_in, gla_w_f2, gla_b_f2,
              gla_norm_g, diff_lam, diff_norm_g, w_branch_a, w_branch_b, w_out):
    n_batch, n_seq = x_prompt.shape[:2]
    dec_batch, dec_seq = x_sample.shape[:2]
    past = page_table.shape[1] * PAGE_SIZE
    pos_p = jnp.arange(n_seq, dtype=jnp.int32)
    pos_s = past + jnp.arange(dec_seq, dtype=jnp.int32)
    hp, hs = x_prompt, x_sample
    kp, vp, sp, kd, vd, sd = [], [], [], [], [], []
    for l in range(DEPTH):
        lw = (w_ada[l], b_ada[l], ln_g[l], ln_b[l], ffn_w_in[l], ffn_w_out[l], w_in[l],
              gla_w_f2[l], gla_b_f2[l], gla_norm_g[l], diff_lam[l], diff_norm_g[l],
              w_branch_a[l], w_branch_b[l], w_out[l])
        s0_p = jnp.zeros((n_batch, GLA_HEADS, GLA_HK, GLA_HV), hp.dtype)
        hp, s_p, k_p, v_p = decoder_layer(hp, c_prompt, pos_p, l, s0_p, None, None, *lw)
        k_past = cache_k[l, page_table].reshape(dec_batch, past, DIFF_HEADS, 2, DIFF_HD)
        v_past = cache_v[l, page_table].reshape(dec_batch, past, DIFF_HEADS, 2 * DIFF_HD)
        hs, s_s, k_s, v_s = decoder_layer(hs, c_sample, pos_s, l, state_gla[l], k_past, v_past, *lw)
        kp.append(k_p); vp.append(v_p); sp.append(s_p)
        kd.append(k_s); vd.append(v_s); sd.append(s_s)
    return (hp, hs, jnp.stack(kp), jnp.stack(vp), jnp.stack(sp),
            jnp.stack(kd), jnp.stack(vd), jnp.stack(sd))
```

```python
import functools
import math

import jax
import jax.numpy as jnp
from jax import lax
from jax.experimental import pallas as pl
from jax.experimental.pallas import tpu as pltpu

F32 = jnp.float32
BF16 = jnp.bfloat16

LN_EPS = 1e-5
GLA_TAU = 16.0
ROPE_THETA = 10000.0
N_MOD = 9
GLA_SUB = 16
VMEM_LIMIT_BYTES = 56 * 1024 * 1024


def _cparams(sem):
    return pltpu.CompilerParams(dimension_semantics=sem, vmem_limit_bytes=VMEM_LIMIT_BYTES)


def _dot(a, b):
    return jnp.dot(a, b, preferred_element_type=F32)


def _dot_nt(a, b):
    return lax.dot_general(a, b, (((1,), (1,)), ((), ())), preferred_element_type=F32)


def _dot_tn(a, b):
    return lax.dot_general(a, b, (((0,), (0,)), ((), ())), preferred_element_type=F32)


def _sigmoid(x):
    return 1.0 / (1.0 + jnp.exp(-x))


def _silu(x):
    return x * _sigmoid(x)


def _layer_norm(y, g, b):
    mu = jnp.mean(y, axis=-1, keepdims=True)
    yc = y - mu
    var = jnp.mean(yc * yc, axis=-1, keepdims=True)
    return yc * lax.rsqrt(var + LN_EPS) * g + b


def _rms_norm(o, g):
    return o * lax.rsqrt(jnp.mean(o * o, axis=-1, keepdims=True) + LN_EPS) * g


def _mod_specs(mod, layer, sub, tm, rows_per_batch, which):
    d = mod.shape[-1] // N_MOD
    specs = []
    for w in which:
        col = 3 * sub + w
        if rows_per_batch is not None:
            tiles = rows_per_batch // tm
            specs.append(pl.BlockSpec((None, None, 1, d),
                                      lambda i, *_, col=col, tiles=tiles: (layer, i // tiles, 0, col)))
        else:
            specs.append(pl.BlockSpec((None, tm, d), lambda i, *_, col=col: (layer, i, col)))
    return specs


def _ada_kernel(c_ref, w_ref, b_ref, o_ref):
    a = _silu(c_ref[...]).astype(BF16)
    o_ref[...] = _dot(a, w_ref[...].astype(BF16)) + b_ref[...]


def _ada_mod(c_all, w_ada, b_ada):
    depth, d, n = w_ada.shape
    r = c_all.shape[0]
    tn = d
    return pl.pallas_call(
        _ada_kernel,
        grid=(depth, n // tn),
        in_specs=[pl.BlockSpec((r, d), lambda l, j: (0, 0)),
                  pl.BlockSpec((None, d, tn), lambda l, j: (l, 0, j)),
                  pl.BlockSpec((None, 1, tn), lambda l, j: (l, 0, j))],
        out_specs=pl.BlockSpec((None, r, tn), lambda l, j: (l, 0, j)),
        out_shape=jax.ShapeDtypeStruct((depth, r, n), F32),
        compiler_params=_cparams(("parallel", "parallel")),
        name="ada_mod",
    )(c_all, w_ada, b_ada.reshape(depth, 1, n))


def _ffn_kernel(x_ref, sh_ref, sc_ref, gt_ref, wa_ref, wb_ref, wo_ref, g_ref, b_ref, o_ref,
                u_scr, acc_scr, *, alpha, nf):
    j = pl.program_id(1)

    @pl.when(j == 0)
    def _():
        u_scr[...] = (x_ref[...] * (1.0 + sc_ref[...]) + sh_ref[...]).astype(BF16)

    u = u_scr[...]
    a = _dot(u, wa_ref[...])
    b = _dot(u, wb_ref[...])
    part = _dot((_silu(a) * b).astype(BF16), wo_ref[...])

    @pl.when(j == 0)
    def _():
        acc_scr[...] = part

    @pl.when(j > 0)
    def _():
        acc_scr[...] += part

    @pl.when(j == nf - 1)
    def _():
        y = alpha * x_ref[...] + gt_ref[...] * (0.5 * acc_scr[...])
        o_ref[...] = _layer_norm(y, g_ref[...], b_ref[...])


def _ffn(x, mod, layer, sub, which_ffn, w_in, w_out, ln_g, ln_b, *, alpha, tm, tf, rows_per_batch):
    m, d = x.shape
    f = w_out.shape[2]
    nf = f // tf
    ln_i = sub
    in_specs = [pl.BlockSpec((tm, d), lambda i, j: (i, 0))]
    in_specs += _mod_specs(mod, layer, sub, tm, rows_per_batch, (0, 1, 2))
    in_specs += [
        pl.BlockSpec((None, None, d, tf), lambda i, j: (layer, which_ffn, 0, j)),
        pl.BlockSpec((None, None, d, tf), lambda i, j: (layer, which_ffn, 0, nf + j)),
        pl.BlockSpec((None, None, tf, d), lambda i, j: (layer, which_ffn, j, 0)),
        pl.BlockSpec((None, None, 1, d), lambda i, j: (layer, ln_i, 0, 0)),
        pl.BlockSpec((None, None, 1, d), lambda i, j: (layer, ln_i, 0, 0)),
    ]
    return pl.pallas_call(
        functools.partial(_ffn_kernel, alpha=alpha, nf=nf),
        grid=(m // tm, nf),
        in_specs=in_specs,
        out_specs=pl.BlockSpec((tm, d), lambda i, j: (i, 0)),
        out_shape=jax.ShapeDtypeStruct((m, d), F32),
        scratch_shapes=[pltpu.VMEM((tm, d), BF16), pltpu.VMEM((tm, d), F32)],
        compiler_params=_cparams(("parallel", "arbitrary")),
        name="ffn",
    )(x, mod, mod, mod, w_in, w_in, w_out, ln_g, ln_b)


def _proj_gla_kernel(x_ref, sh_ref, sc_ref, wg_ref, wf1_ref, wf2_ref, bf2_ref,
                     q_ref, k_ref, v_ref, r_ref, g_ref, *, dk, dv, qscale):
    u = (x_ref[...] * (1.0 + sc_ref[...]) + sh_ref[...]).astype(BF16)
    q_ref[...] = _dot(u, wg_ref[:, 0:dk]) * qscale
    k_ref[...] = _dot(u, wg_ref[:, dk:2 * dk])
    v_ref[...] = _dot(u, wg_ref[:, 2 * dk:2 * dk + dv]).astype(BF16)
    r_ref[...] = _silu(_dot(u, wg_ref[:, 2 * dk + dv:2 * dk + 2 * dv]))
    gf = _dot(u, wf1_ref[...])
    z = _dot(gf.astype(BF16), wf2_ref[...]) + bf2_ref[...]
    g_ref[...] = (jnp.minimum(z, 0.0) - jnp.log1p(jnp.exp(-jnp.abs(z)))) * (1.0 / GLA_TAU)


def _proj_gla(x, mod, layer, wg, wf1, wf2, bf2, *, dk, dv, tm, rows_per_batch):
    m, d = x.shape
    rp = wf1.shape[-1]
    in_specs = [pl.BlockSpec((tm, d), lambda i: (i, 0))]
    in_specs += _mod_specs(mod, layer, 1, tm, rows_per_batch, (0, 1))
    in_specs += [
        pl.BlockSpec((None, d, 2 * dk + 2 * dv), lambda i: (layer, 0, 0)),
        pl.BlockSpec((None, d, rp), lambda i: (layer, 0, 0)),
        pl.BlockSpec((None, rp, dk), lambda i: (layer, 0, 0)),
        pl.BlockSpec((None, 1, dk), lambda i: (layer, 0, 0)),
    ]
    hk_scale = None
    out_shapes = [jax.ShapeDtypeStruct((m, dk), F32), jax.ShapeDtypeStruct((m, dk), F32),
                  jax.ShapeDtypeStruct((m, dv), BF16), jax.ShapeDtypeStruct((m, dv), F32),
                  jax.ShapeDtypeStruct((m, dk), F32)]
    out_specs = [pl.BlockSpec((tm, s.shape[1]), lambda i: (i, 0)) for s in out_shapes]
    del hk_scale
    return in_specs, out_specs, out_shapes


def _proj_diff_kernel(x_ref, sh_ref, sc_ref, wd_ref, cos_ref, sin_ref,
                      q_ref, kf_ref, kb_ref, vf_ref, vb_ref, *, w, hd, qscale):
    u = (x_ref[...] * (1.0 + sc_ref[...]) + sh_ref[...]).astype(BF16)
    cos = cos_ref[...]
    sin = sin_ref[...]
    lane = lax.broadcasted_iota(jnp.int32, cos.shape, 1)
    first_half = (lane % hd) < (hd // 2)

    def rope(t):
        rot = jnp.where(first_half, pltpu.roll(t, 128 - hd // 2, 1), pltpu.roll(t, hd // 2, 1))
        return t * cos + rot * sin

    dq = _dot(u, wd_ref[:, 0:w])
    dk = _dot(u, wd_ref[:, w:2 * w])
    for c in range(w // 128):
        sl = slice(c * 128, (c + 1) * 128)
        q_ref[:, sl] = (rope(dq[:, sl]) * qscale).astype(BF16)
        kr = rope(dk[:, sl])
        kf_ref[:, sl] = kr
        kb_ref[:, sl] = kr.astype(BF16)
    dv = _dot(u, wd_ref[:, 2 * w:3 * w])
    vf_ref[...] = dv
    vb_ref[...] = dv.astype(BF16)


def _proj_diff(x, mod, layer, wd, cos_t, sin_t, *, hd, tm, rows_per_batch):
    m, d = x.shape
    w = wd.shape[-1] // 3
    npos = cos_t.shape[0] // tm
    in_specs = [pl.BlockSpec((tm, d), lambda i: (i, 0))]
    in_specs += _mod_specs(mod, layer, 1, tm, rows_per_batch, (0, 1))
    in_specs += [
        pl.BlockSpec((None, d, 3 * w), lambda i: (layer, 0, 0)),
        pl.BlockSpec((tm, 128), lambda i: (i % npos, 0)),
        pl.BlockSpec((tm, 128), lambda i: (i % npos, 0)),
    ]
    out_shapes = [jax.ShapeDtypeStruct((m, w), BF16), jax.ShapeDtypeStruct((m, w), F32),
                  jax.ShapeDtypeStruct((m, w), BF16), jax.ShapeDtypeStruct((m, w), F32),
                  jax.ShapeDtypeStruct((m, w), BF16)]
    return pl.pallas_call(
        functools.partial(_proj_diff_kernel, w=w, hd=hd, qscale=hd ** -0.5),
        grid=(m // tm,),
        in_specs=in_specs,
        out_specs=[pl.BlockSpec((tm, w), lambda i: (i, 0)) for _ in out_shapes],
        out_shape=out_shapes,
        compiler_params=_cparams(("parallel",)),
        name="proj_diff",
    )(x, mod, mod, wd, cos_t, sin_t)


def _proj_gla_call(x, mod, layer, wg, wf1, wf2, bf2, *, dk, dv, hk, tm, rows_per_batch):
    m, _ = x.shape
    in_specs, out_specs, out_shapes = _proj_gla(x, mod, layer, wg, wf1, wf2, bf2, dk=dk, dv=dv, tm=tm,
                                                rows_per_batch=rows_per_batch)
    return pl.pallas_call(
        functools.partial(_proj_gla_kernel, dk=dk, dv=dv, qscale=hk ** -0.5),
        grid=(m // tm,),
        in_specs=in_specs,
        out_specs=out_specs,
        out_shape=out_shapes,
        compiler_params=_cparams(("parallel",)),
        name="proj_gla",
    )(x, mod, mod, wg, wf1, wf2, bf2)


def _gla_kernel(q_ref, k_ref, g_ref, v_ref, r_ref, s0_ref, ng_ref, o_ref, sout_ref, st_scr,
                *, chunk, nchunks, nt):
    t = pl.program_id(2)
    c_rows = chunk

    @pl.when(t == 0)
    def _():
        st_scr[...] = s0_ref[...].T

    row = lax.broadcasted_iota(jnp.int32, (c_rows, c_rows), 0)
    col = lax.broadcasted_iota(jnp.int32, (c_rows, c_rows), 1)
    tril = (row >= col).astype(F32)
    sub_lane = lax.broadcasted_iota(jnp.int32, (GLA_SUB, c_rows), 1)
    ng = ng_ref[...]

    def body(c, carry):
        rows = pl.ds(pl.multiple_of(c * c_rows, c_rows), c_rows)
        q = q_ref[rows, :]
        k = k_ref[rows, :]
        g = g_ref[rows, :]
        v = v_ref[rows, :]
        cum = jnp.dot(tril, g, precision=lax.Precision.HIGHEST, preferred_element_type=F32)
        st = st_scr[...]
        o = _dot_nt((q * jnp.exp(cum)).astype(BF16), st.astype(BF16))

        blocks = []
        for i in range(c_rows // GLA_SUB):
            lo = i * GLA_SUB
            qi = q[lo:lo + GLA_SUB]
            ci = cum[lo:lo + GLA_SUB]
            acc = jnp.zeros((GLA_SUB, c_rows), F32)
            for s in range(GLA_SUB):
                dec = jnp.exp(jnp.minimum(ci - ci[s:s + 1], 0.0))
                a_col = jnp.sum(qi * k[lo + s:lo + s + 1] * dec, axis=-1, keepdims=True)
                acc = jnp.where(sub_lane == lo + s, a_col, acc)
            if i > 0:
                base = cum[lo - 1:lo]
                qt = (qi * jnp.exp(ci - base)).astype(BF16)
                kt = (k * jnp.exp(jnp.minimum(base - cum, 0.0))).astype(BF16)
                acc = jnp.where(sub_lane < lo, _dot_nt(qt, kt), acc)
            blocks.append(acc)
        att = blocks[0] if len(blocks) == 1 else jnp.concatenate(blocks, axis=0)
        att = jnp.where(row >= col, att, 0.0)
        o = o + _dot(att.astype(BF16), v)

        o_ref[rows, :] = (_rms_norm(o, ng) * r_ref[rows, :]).astype(BF16)

        last = cum[c_rows - 1:c_rows]
        kd = (k * jnp.exp(last - cum)).astype(BF16)
        st_scr[...] = st * jnp.exp(last) + _dot_tn(v, kd)
        return carry

    lax.fori_loop(0, nchunks, body, 0)

    @pl.when(t == nt - 1)
    def _():
        sout_ref[...] = st_scr[...].T


def _gla(q, k, g, v, r, s0, ng, *, rows_per_batch, tile, chunk):
    m = q.shape[0]
    b, h, hk, hv = s0.shape
    nt = rows_per_batch // tile
    assert m == b * rows_per_batch and tile % chunk == 0 and chunk % GLA_SUB == 0
    kspec = pl.BlockSpec((tile, hk), lambda bi, hi, ti: (bi * nt + ti, hi))
    vspec = pl.BlockSpec((tile, hv), lambda bi, hi, ti: (bi * nt + ti, hi))
    sspec = pl.BlockSpec((None, None, hk, hv), lambda bi, hi, ti: (bi, hi, 0, 0))
    return pl.pallas_call(
        functools.partial(_gla_kernel, chunk=chunk, nchunks=tile // chunk, nt=nt),
        grid=(b, h, nt),
        in_specs=[kspec, kspec, kspec, vspec, vspec, sspec,
                  pl.BlockSpec((1, hv), lambda bi, hi, ti: (0, 0))],
        out_specs=[vspec, sspec],
        out_shape=[jax.ShapeDtypeStruct((m, h * hv), BF16), jax.ShapeDtypeStruct((b, h, hk, hv), F32)],
        scratch_shapes=[pltpu.VMEM((hv, hk), F32)],
        compiler_params=_cparams(("parallel", "parallel", "arbitrary")),
        name="gla",
    )(q, k, g, v, r, s0, ng)


def _lambda(lam_ref, lam_init):
    lf = lam_ref[...]
    a = jnp.sum(lf[0:1] * lf[1:2], axis=-1, keepdims=True)
    b = jnp.sum(lf[2:3] * lf[3:4], axis=-1, keepdims=True)
    return jnp.exp(a) - jnp.exp(b) + lam_init


def _attn_prompt_kernel(q_ref, k_ref, v_ref, ng_ref, lam_ref, o_ref, qm_scr, m_scr, l_scr, acc_scr,
                        *, tq, hd, lam_init):
    qi = pl.program_id(2)
    q = q_ref[...]
    lane = lax.broadcasted_iota(jnp.int32, q.shape, 1)
    zero = jnp.zeros_like(q)
    qm_scr[0] = jnp.where(lane < hd, q, zero)
    qm_scr[1] = jnp.where(lane >= hd, q, zero)
    m_scr[...] = jnp.full(m_scr.shape, -jnp.inf, F32)
    l_scr[...] = jnp.zeros(l_scr.shape, F32)
    acc_scr[...] = jnp.zeros(acc_scr.shape, F32)

    def step(j, masked):
        rows = pl.ds(pl.multiple_of(j * tq, tq), tq)
        kb = k_ref[rows, :]
        vb = v_ref[rows, :]
        for c in range(2):
            s = _dot_nt(qm_scr[c], kb)
            if masked:
                r_i = lax.broadcasted_iota(jnp.int32, s.shape, 0)
                c_i = lax.broadcasted_iota(jnp.int32, s.shape, 1)
                s = jnp.where(r_i >= c_i, s, -jnp.inf)
            m_prev = m_scr[c]
            m_new = jnp.maximum(m_prev, jnp.max(s, axis=-1, keepdims=True))
            alpha = jnp.exp(m_prev - m_new)
            p = jnp.exp(s - m_new)
            l_scr[c] = alpha * l_scr[c] + jnp.sum(p, axis=-1, keepdims=True)
            acc_scr[c] = alpha * acc_scr[c] + _dot(p.astype(BF16), vb)
            m_scr[c] = m_new

    def body(j, carry):
        step(j, False)
        return carry

    lax.fori_loop(0, qi, body, 0)
    step(qi, True)

    lam = _lambda(lam_ref, lam_init)
    o = acc_scr[0] / l_scr[0] - lam * (acc_scr[1] / l_scr[1])
    o_ref[...] = (_rms_norm(o, ng_ref[...]) * (1.0 - lam_init)).astype(BF16)


def _attn_prompt(q, k, v, ng, lam_p, *, batch, heads, hd, seq, tq, lam_init):
    m, w = q.shape
    nq = seq // tq
    qspec = pl.BlockSpec((tq, 2 * hd), lambda b, h, i: (b * nq + i, h))
    kvspec = pl.BlockSpec((seq, 2 * hd), lambda b, h, i: (b, h))
    return pl.pallas_call(
        functools.partial(_attn_prompt_kernel, tq=tq, hd=hd, lam_init=lam_init),
        grid=(batch, heads, nq),
        in_specs=[qspec, kvspec, kvspec,
                  pl.BlockSpec((1, 2 * hd), lambda b, h, i: (0, 0)),
                  pl.BlockSpec((4, hd), lambda b, h, i: (0, 0))],
        out_specs=qspec,
        out_shape=jax.ShapeDtypeStruct((m, w), BF16),
        scratch_shapes=[pltpu.VMEM((2, tq, 2 * hd), BF16), pltpu.VMEM((2, tq, 1), F32),
                        pltpu.VMEM((2, tq, 1), F32), pltpu.VMEM((2, tq, 2 * hd), F32)],
        compiler_params=_cparams(("parallel", "parallel", "arbitrary")),
        name="attn_prompt",
    )(q, k, v, ng, lam_p)


def _attn_decode_kernel(pt_ref, qbd_ref, kn_ref, vn_ref, ng_ref, lam_ref, *refs, pg, dec_seq, heads, hd, lam_init):
    del pt_ref
    k_refs = refs[:pg]
    v_refs = refs[pg:2 * pg]
    o_ref, m_scr, l_scr, acc_scr = refs[2 * pg:]
    j = pl.program_id(1)
    nj = pl.num_programs(1)
    qbd = qbd_ref[...]
    rows = qbd.shape[0]

    @pl.when(j == 0)
    def _():
        s = _dot_nt(qbd, kn_ref[...])
        r_i = lax.broadcasted_iota(jnp.int32, s.shape, 0) % dec_seq
        c_i = lax.broadcasted_iota(jnp.int32, s.shape, 1)
        s = jnp.where(c_i <= r_i, s, -jnp.inf)
        m0 = jnp.max(s, axis=-1, keepdims=True)
        p = jnp.exp(s - m0)
        m_scr[...] = m0
        l_scr[...] = jnp.sum(p, axis=-1, keepdims=True)
        acc_scr[...] = _dot(p.astype(BF16), vn_ref[...])

    s = jnp.concatenate([_dot_nt(qbd, kr[...].astype(BF16)) for kr in k_refs], axis=-1)
    page = s.shape[1] // pg
    m_prev = m_scr[...]
    m_new = jnp.maximum(m_prev, jnp.max(s, axis=-1, keepdims=True))
    alpha = jnp.exp(m_prev - m_new)
    p = jnp.exp(s - m_new).astype(BF16)
    l_scr[...] = alpha * l_scr[...] + jnp.sum(p.astype(F32), axis=-1, keepdims=True)
    pv = _dot(p[:, 0:page], v_refs[0][...].astype(BF16))
    for i in range(1, pg):
        pv = pv + _dot(p[:, i * page:(i + 1) * page], v_refs[i][...].astype(BF16))
    acc_scr[...] = alpha * acc_scr[...] + pv
    m_scr[...] = m_new

    @pl.when(j == nj - 1)
    def _():
        lam = _lambda(lam_ref, lam_init)
        ng = ng_ref[...]
        grp = 2 * dec_seq
        for h in range(heads):
            blk = acc_scr[h * grp:(h + 1) * grp, h * 2 * hd:(h + 1) * 2 * hd] / l_scr[h * grp:(h + 1) * grp, :]
            o = blk[0:dec_seq] - lam * blk[dec_seq:grp]
            o_ref[:, h * 2 * hd:(h + 1) * 2 * hd] = (_rms_norm(o, ng) * (1.0 - lam_init)).astype(BF16)
    del rows


def _attn_decode(page_table, qbd, k_new, v_new, ng, lam_p, cache_k, cache_v, layer, *, pg, dec_seq, heads, hd,
                 lam_init):
    db, r, w = qbd.shape
    n_pages = page_table.shape[1]
    page = cache_k.shape[2]
    assert n_pages % pg == 0

    def page_spec(i):
        return pl.BlockSpec((None, None, page, w), lambda b, j, pt, i=i: (layer, pt[b, j * pg + i], 0, 0))

    in_specs = [pl.BlockSpec((None, r, w), lambda b, j, pt: (b, 0, 0)),
                pl.BlockSpec((None, page, w), lambda b, j, pt: (b, 0, 0)),
                pl.BlockSpec((None, page, w), lambda b, j, pt: (b, 0, 0)),
                pl.BlockSpec((1, 2 * hd), lambda b, j, pt: (0, 0)),
                pl.BlockSpec((4, hd), lambda b, j, pt: (0, 0))]
    in_specs += [page_spec(i) for i in range(pg)] * 2
    return pl.pallas_call(
        functools.partial(_attn_decode_kernel, pg=pg, dec_seq=dec_seq, heads=heads, hd=hd, lam_init=lam_init),
        grid_spec=pltpu.PrefetchScalarGridSpec(
            num_scalar_prefetch=1,
            grid=(db, n_pages // pg),
            in_specs=in_specs,
            out_specs=pl.BlockSpec((None, dec_seq, w), lambda b, j, pt: (b, 0, 0)),
            scratch_shapes=[pltpu.VMEM((r, 1), F32), pltpu.VMEM((r, 1), F32), pltpu.VMEM((r, w), F32)]),
        out_shape=jax.ShapeDtypeStruct((db, dec_seq, w), BF16),
        compiler_params=_cparams(("parallel", "arbitrary")),
        name="attn_decode",
    )(page_table, qbd, k_new, v_new, ng, lam_p, *([cache_k] * pg), *([cache_v] * pg))


def _mix_kernel(x_ref, sh_ref, sc_ref, gt_ref, oa_ref, ob_ref, wga_ref, wgb_ref, wa_ref, wb_ref, wo_ref,
                g_ref, b_ref, o_ref, *, alpha):
    x = x_ref[...]
    u = (x * (1.0 + sc_ref[...]) + sh_ref[...]).astype(BF16)
    ya = _sigmoid(_dot(u, wga_ref[...])) * _dot(oa_ref[...], wa_ref[...])
    yb = _sigmoid(_dot(u, wgb_ref[...])) * _dot(ob_ref[...], wb_ref[...])
    z = _dot((ya + yb).astype(BF16), wo_ref[...])
    o_ref[...] = _layer_norm(alpha * x + gt_ref[...] * z, g_ref[...], b_ref[...])


def _mix(x, mod, layer, oa, ob, wgate, wa, wb, wo, ln_g, ln_b, *, alpha, tm, rows_per_batch):
    m, d = x.shape
    row = pl.BlockSpec((tm, d), lambda i: (i, 0))
    wsq = pl.BlockSpec((None, d, d), lambda i: (layer, 0, 0))
    in_specs = [row] + _mod_specs(mod, layer, 1, tm, rows_per_batch, (0, 1, 2)) + [
        row, row,
        pl.BlockSpec((None, d, d), lambda i: (layer, 0, 0)),
        pl.BlockSpec((None, d, d), lambda i: (layer, 0, 1)),
        wsq, wsq, wsq,
        pl.BlockSpec((None, None, 1, d), lambda i: (layer, 1, 0, 0)),
        pl.BlockSpec((None, None, 1, d), lambda i: (layer, 1, 0, 0)),
    ]
    return pl.pallas_call(
        functools.partial(_mix_kernel, alpha=alpha),
        grid=(m // tm,),
        in_specs=in_specs,
        out_specs=row,
        out_shape=jax.ShapeDtypeStruct((m, d), F32),
        compiler_params=_cparams(("parallel",)),
        name="mix",
    )(x, mod, mod, mod, oa, ob, wgate, wgate, wa, wb, wo, ln_g, ln_b)


def _rope_tables(pos, hd):
    half = hd // 2
    inv = ROPE_THETA ** (-jnp.arange(half, dtype=F32) * (2.0 / hd))
    ang = pos.astype(F32)[:, None] * inv[None, :]
    cos = jnp.cos(ang)
    sin = jnp.sin(ang)
    reps = 128 // hd
    cos_t = jnp.tile(jnp.concatenate([cos, cos], -1), (1, reps))
    sin_t = jnp.tile(jnp.concatenate([-sin, sin], -1), (1, reps))
    return cos_t, sin_t


def _tile(n, pref):
    t = min(n, pref)
    while n % t:
        t //= 2
    return t


def _ffn_tile(f):
    best = 128
    for t in range(128, min(f, 1536) + 1, 128):
        if f % t == 0:
            best = t
    return best


def kernel(x_prompt, x_sample, cache_k, cache_v, state_gla, page_table, c_prompt, c_sample, w_ada, b_ada,
           ln_g, ln_b, ffn_w_in, ffn_w_out, w_in, gla_w_f2, gla_b_f2, gla_norm_g, diff_lam, diff_norm_g,
           w_branch_a, w_branch_b, w_out):
    batch, seq, d = x_prompt.shape
    db, dec_seq, _ = x_sample.shape
    depth = w_ada.shape[0]
    gla_heads, hk, hv = state_gla.shape[2:]
    n_pool, page, heads, _, hd = cache_k.shape[1:]
    dk, dv = gla_heads * hk, gla_heads * hv
    wq = 2 * heads * hd
    rank = gla_w_f2.shape[1]
    f = ffn_w_out.shape[2]
    past = page_table.shape[1] * page
    alpha = (2.0 * depth) ** 0.25
    mp, ms = batch * seq, db * dec_seq

    offs = [0, dk, 2 * dk, 2 * dk + dv, 2 * dk + 2 * dv]
    o_f = offs[-1]
    o_dq = o_f + rank
    o_ga = o_dq + 3 * wq
    wg = w_in[:, :, 0:o_f].astype(BF16)
    rank_pad = -(-rank // 128) * 128
    wf1 = jnp.pad(w_in[:, :, o_f:o_dq], ((0, 0), (0, 0), (0, rank_pad - rank))).astype(BF16)
    wf2 = jnp.pad(gla_w_f2, ((0, 0), (0, rank_pad - rank), (0, 0))).astype(BF16)
    bf2 = gla_b_f2.reshape(depth, 1, dk)
    wd = w_in[:, :, o_dq:o_ga].astype(BF16)
    wgate = w_in[:, :, o_ga:o_ga + 2 * d].astype(BF16)
    ffn_in = ffn_w_in.astype(BF16)
    ffn_out = ffn_w_out.astype(BF16)
    wa = w_branch_a.astype(BF16)
    wb = w_branch_b.astype(BF16)
    wo = w_out.astype(BF16)
    ln_g4 = ln_g.reshape(depth, 3, 1, d)
    ln_b4 = ln_b.reshape(depth, 3, 1, d)
    gla_ng = gla_norm_g.reshape(depth, 1, hv)
    diff_ng = diff_norm_g.reshape(depth, 1, 2 * hd)

    n_c = batch + db
    n_c_pad = -(-n_c // 8) * 8
    c_all = jnp.pad(jnp.concatenate([c_prompt, c_sample], 0), ((0, n_c_pad - n_c), (0, 0)))
    mod = _ada_mod(c_all, w_ada, b_ada)
    mod_p = mod[:, :batch].reshape(depth, batch, 1, N_MOD * d)
    mod_s = jnp.repeat(mod[:, batch:n_c], dec_seq, axis=1)

    cos_p, sin_p = _rope_tables(jnp.arange(seq, dtype=jnp.int32), hd)
    cos_s, sin_s = _rope_tables(past + jnp.arange(dec_seq, dtype=jnp.int32), hd)
    cos_s = jnp.tile(cos_s, (db, 1))
    sin_s = jnp.tile(sin_s, (db, 1))

    tm_p = _tile(seq, 512)
    tf = _ffn_tile(f)
    tq = _tile(seq, 512)
    gla_tile = _tile(seq, 512)
    gla_chunk = _tile(gla_tile, 64)
    dec_pad = -(-dec_seq // GLA_SUB) * GLA_SUB
    pg = _tile(page_table.shape[1], 8)
    cache_k4 = cache_k.reshape(depth, n_pool, page, wq)
    cache_v4 = cache_v.reshape(depth, n_pool, page, wq)
    eye_hc = jnp.eye(2 * heads, dtype=BF16)

    hp = x_prompt.reshape(mp, d)
    hs = x_sample.reshape(ms, d)
    zeros_state = jnp.zeros((batch, gla_heads, hk, hv), F32)
    outs = {n: [] for n in ("kp", "vp", "sp", "kd", "vd", "sd")}

    def pad_rows(t):
        t3 = t.reshape(db, dec_seq, t.shape[-1])
        return jnp.pad(t3, ((0, 0), (0, dec_pad - dec_seq), (0, 0))).reshape(db * dec_pad, t.shape[-1])

    for l in range(depth):
        lam_init = 0.8 - 0.6 * math.exp(-0.3 * l)

        hp = _ffn(hp, mod_p, l, 0, 0, ffn_in, ffn_out, ln_g4, ln_b4, alpha=alpha, tm=tm_p, tf=tf,
                  rows_per_batch=seq)
        gq, gk, gv, gr, gg = _proj_gla_call(hp, mod_p, l, wg, wf1, wf2, bf2, dk=dk, dv=dv, hk=hk, tm=tm_p,
                                            rows_per_batch=seq)
        dq, kf, kb, vf, vb = _proj_diff(hp, mod_p, l, wd, cos_p, sin_p, hd=hd, tm=tm_p, rows_per_batch=seq)
        oa, s_p = _gla(gq, gk, gg, gv, gr, zeros_state, gla_ng[l], rows_per_batch=seq, tile=gla_tile,
                       chunk=gla_chunk)
        ob = _attn_prompt(dq, kb, vb, diff_ng[l], diff_lam[l], batch=batch, heads=heads, hd=hd, seq=seq, tq=tq,
                          lam_init=lam_init)
        hp = _mix(hp, mod_p, l, oa, ob, wgate, wa, wb, wo, ln_g4, ln_b4, alpha=alpha, tm=tm_p,
                  rows_per_batch=seq)
        hp = _ffn(hp, mod_p, l, 2, 1, ffn_in, ffn_out, ln_g4, ln_b4, alpha=alpha, tm=tm_p, tf=tf,
                  rows_per_batch=seq)
        outs["kp"].append(kf)
        outs["vp"].append(vf)
        outs["sp"].append(s_p)

        hs = _ffn(hs, mod_s, l, 0, 0, ffn_in, ffn_out, ln_g4, ln_b4, alpha=alpha, tm=ms, tf=tf,
                  rows_per_batch=None)
        gq, gk, gv, gr, gg = _proj_gla_call(hs, mod_s, l, wg, wf1, wf2, bf2, dk=dk, dv=dv, hk=hk, tm=ms,
                                            rows_per_batch=None)
        dq, kf, kb, vf, vb = _proj_diff(hs, mod_s, l, wd, cos_s, sin_s, hd=hd, tm=ms, rows_per_batch=None)
        oa, s_s = _gla(pad_rows(gq), pad_rows(gk), pad_rows(gg), pad_rows(gv), pad_rows(gr), state_gla[l],
                       gla_ng[l], rows_per_batch=dec_pad, tile=dec_pad, chunk=dec_pad)
        oa = oa.reshape(db, dec_pad, dv)[:, :dec_seq].reshape(ms, dv)
        q5 = dq.reshape(db, dec_seq, 2 * heads, hd)
        qbd = (q5.transpose(0, 2, 1, 3)[:, :, :, None, :] * eye_hc[None, :, None, :, None]).reshape(
            db, 2 * heads * dec_seq, wq)
        k_new = jnp.pad(kb.reshape(db, dec_seq, wq), ((0, 0), (0, page - dec_seq), (0, 0)))
        v_new = jnp.pad(vb.reshape(db, dec_seq, wq), ((0, 0), (0, page - dec_seq), (0, 0)))
        ob = _attn_decode(page_table, qbd, k_new, v_new, diff_ng[l], diff_lam[l], cache_k4, cache_v4, l, pg=pg,
                          dec_seq=dec_seq, heads=heads, hd=hd, lam_init=lam_init).reshape(ms, wq)
        hs = _mix(hs, mod_s, l, oa, ob, wgate, wa, wb, wo, ln_g4, ln_b4, alpha=alpha, tm=ms,
                  rows_per_batch=None)
        hs = _ffn(hs, mod_s, l, 2, 1, ffn_in, ffn_out, ln_g4, ln_b4, alpha=alpha, tm=ms, tf=tf,
                  rows_per_batch=None)
        outs["kd"].append(kf)
        outs["vd"].append(vf)
        outs["sd"].append(s_s)

    return (hp.reshape(batch, seq, d),
            hs.reshape(db, dec_seq, d),
            jnp.stack(outs["kp"]).reshape(depth, batch, seq, heads, 2, hd),
            jnp.stack(outs["vp"]).reshape(depth, batch, seq, heads, 2 * hd),
            jnp.stack(outs["sp"]),
            jnp.stack(outs["kd"]).reshape(depth, db, dec_seq, heads, 2, hd),
            jnp.stack(outs["vd"]).reshape(depth, db, dec_seq, heads, 2 * hd),
            jnp.stack(outs["sd"]))
```

```python
import functools
import math

import jax
import jax.numpy as jnp
from jax import lax
from jax.experimental import pallas as pl
from jax.experimental.pallas import tpu as pltpu

F32 = jnp.float32
BF16 = jnp.bfloat16

LN_EPS = 1e-5
GLA_TAU = 16.0
ROPE_THETA = 10000.0
N_MOD = 9
LANES = 128
GLA_SUB = 16
VMEM_LIMIT_BYTES = 56 * 1024 * 1024
LOG2E = math.log2(math.e)


def _cparams(sem):
    return pltpu.CompilerParams(dimension_semantics=sem, vmem_limit_bytes=VMEM_LIMIT_BYTES)


def _dot(a, b):
    return jnp.dot(a, b, preferred_element_type=F32)


def _dot_nt(a, b):
    return lax.dot_general(a, b, (((1,), (1,)), ((), ())), preferred_element_type=F32)


def _dot_tn(a, b):
    return lax.dot_general(a, b, (((0,), (0,)), ((), ())), preferred_element_type=F32)


def _sigmoid(x):
    return 1.0 / (1.0 + jnp.exp(-x))


def _silu(x):
    return x * _sigmoid(x)


def _layer_norm(y, g, b):
    mu = jnp.mean(y, axis=-1, keepdims=True)
    yc = y - mu
    var = jnp.mean(yc * yc, axis=-1, keepdims=True)
    return yc * lax.rsqrt(var + LN_EPS) * g + b


def _rms_norm(o, g):
    return o * lax.rsqrt(jnp.mean(o * o, axis=-1, keepdims=True) + LN_EPS) * g


def _modulated(x_ref, sh_ref, sc_ref):
    return (x_ref[...] * (1.0 + sc_ref[...]) + sh_ref[...]).astype(BF16)


def _mod_specs(mod, layer, sub, tm, rows_per_batch, which):
    d = mod.shape[-1] // N_MOD
    specs = []
    for w in which:
        col = 3 * sub + w
        if rows_per_batch is not None:
            tiles = rows_per_batch // tm
            specs.append(pl.BlockSpec((None, None, 1, d),
                                      lambda i, *_, col=col, tiles=tiles: (layer, i // tiles, 0, col)))
        else:
            specs.append(pl.BlockSpec((None, tm, d), lambda i, *_, col=col: (layer, i, col)))
    return specs


def _ada_kernel(c_ref, w_ref, b_ref, o_ref):
    a = _silu(c_ref[...]).astype(BF16)
    o_ref[...] = _dot(a, w_ref[...].astype(BF16)) + b_ref[...]


def _ada_mod(c_all, w_ada, b_ada):
    depth, d, n = w_ada.shape
    r = c_all.shape[0]
    tn = d
    return pl.pallas_call(
        _ada_kernel,
        grid=(depth, n // tn),
        in_specs=[pl.BlockSpec((r, d), lambda l, j: (0, 0)),
                  pl.BlockSpec((None, d, tn), lambda l, j: (l, 0, j)),
                  pl.BlockSpec((None, 1, tn), lambda l, j: (l, 0, j))],
        out_specs=pl.BlockSpec((None, r, tn), lambda l, j: (l, 0, j)),
        out_shape=jax.ShapeDtypeStruct((depth, r, n), F32),
        compiler_params=_cparams(("parallel", "parallel")),
        name="ada_mod",
    )(c_all, w_ada, b_ada.reshape(depth, 1, n))


def _ffn_kernel(x_ref, sh_ref, sc_ref, gt_ref, wa_ref, wb_ref, wo_ref, g_ref, b_ref, o_ref,
                u_scr, acc_scr, *, alpha, nf):
    j = pl.program_id(1)

    @pl.when(j == 0)
    def _():
        u_scr[...] = _modulated(x_ref, sh_ref, sc_ref)

    u = u_scr[...]
    a = _dot(u, wa_ref[...])
    b = _dot(u, wb_ref[...])
    part = _dot((_silu(a) * b).astype(BF16), wo_ref[...])

    @pl.when(j == 0)
    def _():
        acc_scr[...] = part

    @pl.when(j > 0)
    def _():
        acc_scr[...] += part

    @pl.when(j == nf - 1)
    def _():
        y = alpha * x_ref[...] + gt_ref[...] * (0.5 * acc_scr[...])
        o_ref[...] = _layer_norm(y, g_ref[...], b_ref[...])


def _ffn(x, mod, layer, sub, which_ffn, w_in, w_out, ln_g, ln_b, *, alpha, tm, tf, rows_per_batch):
    m, d = x.shape
    f = w_out.shape[2]
    nf = f // tf
    in_specs = [pl.BlockSpec((tm, d), lambda i, j: (i, 0))]
    in_specs += _mod_specs(mod, layer, sub, tm, rows_per_batch, (0, 1, 2))
    in_specs += [
        pl.BlockSpec((None, None, d, tf), lambda i, j: (layer, which_ffn, 0, j)),
        pl.BlockSpec((None, None, d, tf), lambda i, j: (layer, which_ffn, 0, nf + j)),
        pl.BlockSpec((None, None, tf, d), lambda i, j: (layer, which_ffn, j, 0)),
        pl.BlockSpec((None, None, 1, d), lambda i, j: (layer, sub, 0, 0)),
        pl.BlockSpec((None, None, 1, d), lambda i, j: (layer, sub, 0, 0)),
    ]
    return pl.pallas_call(
        functools.partial(_ffn_kernel, alpha=alpha, nf=nf),
        grid=(m // tm, nf),
        in_specs=in_specs,
        out_specs=pl.BlockSpec((tm, d), lambda i, j: (i, 0)),
        out_shape=jax.ShapeDtypeStruct((m, d), F32),
        scratch_shapes=[pltpu.VMEM((tm, d), BF16), pltpu.VMEM((tm, d), F32)],
        compiler_params=_cparams(("parallel", "arbitrary")),
        name="ffn",
    )(x, mod, mod, mod, w_in, w_in, w_out, ln_g, ln_b)


def _proj_gla_kernel(x_ref, sh_ref, sc_ref, wg_ref, wf1_ref, wf2_ref, bf2_ref,
                     q_ref, k_ref, v_ref, r_ref, g_ref, *, dk, dv, qscale):
    u = _modulated(x_ref, sh_ref, sc_ref)
    q_ref[...] = _dot(u, wg_ref[:, 0:dk]) * qscale
    k_ref[...] = _dot(u, wg_ref[:, dk:2 * dk])
    v_ref[...] = _dot(u, wg_ref[:, 2 * dk:2 * dk + dv]).astype(BF16)
    r_ref[...] = _silu(_dot(u, wg_ref[:, 2 * dk + dv:2 * dk + 2 * dv]))
    gf = _dot(u, wf1_ref[...])
    z = _dot(gf.astype(BF16), wf2_ref[...]) + bf2_ref[...]
    g_ref[...] = (jnp.minimum(z, 0.0) - jnp.log1p(jnp.exp(-jnp.abs(z)))) * (1.0 / GLA_TAU)


def _proj_gla(x, mod, layer, wg, wf1, wf2, bf2, *, dk, dv, hk, tm, rows_per_batch):
    m, d = x.shape
    rp = wf1.shape[-1]
    in_specs = [pl.BlockSpec((tm, d), lambda i: (i, 0))]
    in_specs += _mod_specs(mod, layer, 1, tm, rows_per_batch, (0, 1))
    in_specs += [
        pl.BlockSpec((None, d, 2 * dk + 2 * dv), lambda i: (layer, 0, 0)),
        pl.BlockSpec((None, d, rp), lambda i: (layer, 0, 0)),
        pl.BlockSpec((None, rp, dk), lambda i: (layer, 0, 0)),
        pl.BlockSpec((None, 1, dk), lambda i: (layer, 0, 0)),
    ]
    out_shapes = [jax.ShapeDtypeStruct((m, dk), F32), jax.ShapeDtypeStruct((m, dk), F32),
                  jax.ShapeDtypeStruct((m, dv), BF16), jax.ShapeDtypeStruct((m, dv), F32),
                  jax.ShapeDtypeStruct((m, dk), F32)]
    return pl.pallas_call(
        functools.partial(_proj_gla_kernel, dk=dk, dv=dv, qscale=hk ** -0.5),
        grid=(m // tm,),
        in_specs=in_specs,
        out_specs=[pl.BlockSpec((tm, s.shape[1]), lambda i: (i, 0)) for s in out_shapes],
        out_shape=out_shapes,
        compiler_params=_cparams(("parallel",)),
        name="proj_gla",
    )(x, mod, mod, wg, wf1, wf2, bf2)


def _rope_lanes(t, cos, sin, first_half, hd):
    rot = jnp.where(first_half, pltpu.roll(t, LANES - hd // 2, 1), pltpu.roll(t, hd // 2, 1))
    return t * cos + rot * sin


def _proj_q(u, wd_ref, cos, sin, first_half, q_ref, w, hd, qscale):
    dq = _dot(u, wd_ref[:, 0:w])
    for c in range(w // LANES):
        sl = slice(c * LANES, (c + 1) * LANES)
        q_ref[:, sl] = (_rope_lanes(dq[:, sl], cos, sin, first_half, hd) * qscale).astype(BF16)


def _proj_diff_rows_kernel(x_ref, sh_ref, sc_ref, wd_ref, cos_ref, sin_ref,
                           q_ref, kf_ref, kb_ref, vf_ref, *, w, hd, qscale):
    u = _modulated(x_ref, sh_ref, sc_ref)
    cos = cos_ref[...]
    sin = sin_ref[...]
    first_half = (lax.broadcasted_iota(jnp.int32, cos.shape, 1) % hd) < (hd // 2)
    _proj_q(u, wd_ref, cos, sin, first_half, q_ref, w, hd, qscale)
    dk = _dot(u, wd_ref[:, w:2 * w])
    for c in range(w // LANES):
        sl = slice(c * LANES, (c + 1) * LANES)
        kr = _rope_lanes(dk[:, sl], cos, sin, first_half, hd)
        kf_ref[:, sl] = kr
        kb_ref[:, sl] = kr.astype(BF16)
    vf_ref[...] = _dot(u, wd_ref[:, 2 * w:3 * w])


def _proj_diff_cols_kernel(x_ref, sh_ref, sc_ref, wd_ref, wkt_ref, cos_ref, sin_ref, cost_ref, sint_ref,
                           k_all_ref, v_all_ref, kt_ref, vf_ref, q_ref, ktb_ref, vb_ref,
                           *, w, hd, heads, qscale):
    del k_all_ref, v_all_ref
    u = _modulated(x_ref, sh_ref, sc_ref)
    cos = cos_ref[...]
    sin = sin_ref[...]
    first_half = (lax.broadcasted_iota(jnp.int32, cos.shape, 1) % hd) < (hd // 2)
    _proj_q(u, wd_ref, cos, sin, first_half, q_ref, w, hd, qscale)
    dkt = _dot_nt(wkt_ref[...], u)
    cost = cost_ref[...]
    sint = sint_ref[...]
    half = hd // 2
    for g in range(w // hd):
        x1 = dkt[g * hd:g * hd + half]
        x2 = dkt[g * hd + half:(g + 1) * hd]
        r1 = x1 * cost - x2 * sint
        r2 = x2 * cost + x1 * sint
        kt_ref[g * hd:g * hd + half, :] = r1
        kt_ref[g * hd + half:(g + 1) * hd, :] = r2
        ktb_ref[g * hd:g * hd + half, :] = r1.astype(BF16)
        ktb_ref[g * hd + half:(g + 1) * hd, :] = r2.astype(BF16)
    dv = _dot(u, wd_ref[:, 2 * w:3 * w])
    tm = dv.shape[0]
    for h in range(heads):
        vf_ref[pl.ds(h, tm, stride=heads), :] = dv[:, h * 2 * hd:(h + 1) * 2 * hd]
    vb_ref[...] = dv.astype(BF16)


def _proj_diff_rows(x, mod, layer, wd, cos_t, sin_t, *, hd, tm):
    m, d = x.shape
    w = wd.shape[-1] // 3
    in_specs = [pl.BlockSpec((tm, d), lambda i: (i, 0))]
    in_specs += _mod_specs(mod, layer, 1, tm, None, (0, 1))
    in_specs += [
        pl.BlockSpec((None, d, 3 * w), lambda i: (layer, 0, 0)),
        pl.BlockSpec((tm, LANES), lambda i: (i, 0)),
        pl.BlockSpec((tm, LANES), lambda i: (i, 0)),
    ]
    out_shapes = [jax.ShapeDtypeStruct((m, w), BF16), jax.ShapeDtypeStruct((m, w), F32),
                  jax.ShapeDtypeStruct((m, w), BF16), jax.ShapeDtypeStruct((m, w), F32)]
    return pl.pallas_call(
        functools.partial(_proj_diff_rows_kernel, w=w, hd=hd, qscale=hd ** -0.5 * LOG2E),
        grid=(m // tm,),
        in_specs=in_specs,
        out_specs=[pl.BlockSpec((tm, w), lambda i: (i, 0)) for _ in out_shapes],
        out_shape=out_shapes,
        compiler_params=_cparams(("parallel",)),
        name="proj_diff_rows",
    )(x, mod, mod, wd, cos_t, sin_t)


def _proj_diff_cols(x, mod, layer, wd, wkt, cos_t, sin_t, cos_tt, sin_tt, k_all, v_all, *, hd, heads, tm, seq):
    m, d = x.shape
    w = wkt.shape[1]
    npos = seq // tm
    batch = m // seq
    in_specs = [pl.BlockSpec((tm, d), lambda i: (i, 0))]
    in_specs += _mod_specs(mod, layer, 1, tm, seq, (0, 1))
    in_specs += [
        pl.BlockSpec((None, d, 3 * w), lambda i: (layer, 0, 0)),
        pl.BlockSpec((None, w, d), lambda i: (layer, 0, 0)),
        pl.BlockSpec((tm, LANES), lambda i: (i % npos, 0)),
        pl.BlockSpec((tm, LANES), lambda i: (i % npos, 0)),
        pl.BlockSpec((hd // 2, tm), lambda i: (0, i % npos)),
        pl.BlockSpec((hd // 2, tm), lambda i: (0, i % npos)),
        pl.BlockSpec(memory_space=pl.ANY),
        pl.BlockSpec(memory_space=pl.ANY),
    ]
    out_shapes = [jax.ShapeDtypeStruct(k_all.shape, F32), jax.ShapeDtypeStruct(v_all.shape, F32),
                  jax.ShapeDtypeStruct((m, w), BF16),
                  jax.ShapeDtypeStruct((batch, w, seq), BF16), jax.ShapeDtypeStruct((m, w), BF16)]
    out_specs = [
        pl.BlockSpec((None, None, w, tm), lambda i: (layer, i // npos, 0, i % npos)),
        pl.BlockSpec((None, tm * heads, 2 * hd), lambda i: (layer, i, 0)),
        pl.BlockSpec((tm, w), lambda i: (i, 0)),
        pl.BlockSpec((None, w, tm), lambda i: (i // npos, 0, i % npos)),
        pl.BlockSpec((tm, w), lambda i: (i, 0)),
    ]
    return pl.pallas_call(
        functools.partial(_proj_diff_cols_kernel, w=w, hd=hd, heads=heads, qscale=hd ** -0.5 * LOG2E),
        grid=(m // tm,),
        in_specs=in_specs,
        out_specs=out_specs,
        out_shape=out_shapes,
        input_output_aliases={9: 0, 10: 1},
        compiler_params=_cparams(("parallel",)),
        name="proj_diff_cols",
    )(x, mod, mod, wd, wkt, cos_t, sin_t, cos_tt, sin_tt, k_all, v_all)


def _gla_chunk(q, k, g, v, r, st, ng, tril, causal, sub_lane):
    c_rows = q.shape[0]
    cum = jnp.dot(tril, g, precision=lax.Precision.HIGHEST, preferred_element_type=F32)
    o = _dot_nt((q * jnp.exp(cum)).astype(BF16), st.astype(BF16))

    blocks = []
    for i in range(c_rows // GLA_SUB):
        lo = i * GLA_SUB
        qi = q[lo:lo + GLA_SUB]
        ci = cum[lo:lo + GLA_SUB]
        acc = jnp.zeros((GLA_SUB, c_rows), F32)
        for s in range(GLA_SUB):
            dec = jnp.exp(jnp.minimum(ci - ci[s:s + 1], 0.0))
            a_col = jnp.sum(qi * k[lo + s:lo + s + 1] * dec, axis=-1, keepdims=True)
            acc = jnp.where(sub_lane == lo + s, a_col, acc)
        if i > 0:
            base = cum[lo - 1:lo]
            qt = (qi * jnp.exp(ci - base)).astype(BF16)
            kt = (k * jnp.exp(jnp.minimum(base - cum, 0.0))).astype(BF16)
            acc = jnp.where(sub_lane < lo, _dot_nt(qt, kt), acc)
        blocks.append(acc)
    att = blocks[0] if len(blocks) == 1 else jnp.concatenate(blocks, axis=0)
    att = jnp.where(causal, att, 0.0)
    o = o + _dot(att.astype(BF16), v)
    out = (_rms_norm(o, ng) * r).astype(BF16)

    last = cum[c_rows - 1:c_rows]
    kd = (k * jnp.exp(last - cum)).astype(BF16)
    return out, st * jnp.exp(last) + _dot_tn(v, kd)


def _gla_kernel(*refs, chunk, nchunks, nt, heads, hk, hv, has_s0):
    if has_s0:
        q_ref, k_ref, g_ref, v_ref, r_ref, ng_ref, s0_ref, sall_ref, o_ref, sout_ref, st_scr = refs
    else:
        q_ref, k_ref, g_ref, v_ref, r_ref, ng_ref, sall_ref, o_ref, sout_ref, st_scr = refs
    del sall_ref
    t = pl.program_id(1)

    @pl.when(t == 0)
    def _():
        for h in range(heads):
            st_scr[h] = s0_ref[h].T if has_s0 else jnp.zeros((hv, hk), F32)

    row = lax.broadcasted_iota(jnp.int32, (chunk, chunk), 0)
    col = lax.broadcasted_iota(jnp.int32, (chunk, chunk), 1)
    causal = row >= col
    tril = causal.astype(F32)
    sub_lane = lax.broadcasted_iota(jnp.int32, (GLA_SUB, chunk), 1)
    ng = ng_ref[...]

    def body(c, carry):
        rows = pl.ds(pl.multiple_of(c * chunk, chunk), chunk)
        for h in range(heads):
            kc = slice(h * hk, (h + 1) * hk)
            vc = slice(h * hv, (h + 1) * hv)
            out, st_new = _gla_chunk(q_ref[rows, kc], k_ref[rows, kc], g_ref[rows, kc], v_ref[rows, vc],
                                     r_ref[rows, vc], st_scr[h], ng, tril, causal, sub_lane)
            o_ref[rows, vc] = out
            st_scr[h] = st_new
        return carry

    lax.fori_loop(0, nchunks, body, 0)

    @pl.when(t == nt - 1)
    def _():
        for h in range(heads):
            sout_ref[h] = st_scr[h].T


def _gla(q, k, g, v, r, ng, s_all, layer, s0_all=None, *, rows_per_batch, tile, chunk):
    m = q.shape[0]
    _, b, h, hk, hv = s_all.shape
    nt = rows_per_batch // tile
    assert m == b * rows_per_batch and tile % chunk == 0 and chunk % GLA_SUB == 0
    kspec = pl.BlockSpec((tile, h * hk), lambda bi, ti: (bi * nt + ti, 0))
    vspec = pl.BlockSpec((tile, h * hv), lambda bi, ti: (bi * nt + ti, 0))
    sspec = pl.BlockSpec((None, None, h, hk, hv), lambda bi, ti: (layer, bi, 0, 0, 0))
    in_specs = [kspec, kspec, kspec, vspec, vspec, pl.BlockSpec((1, hv), lambda bi, ti: (0, 0))]
    args = [q, k, g, v, r, ng]
    if s0_all is not None:
        in_specs.append(sspec)
        args.append(s0_all)
    in_specs.append(pl.BlockSpec(memory_space=pl.ANY))
    args.append(s_all)
    return pl.pallas_call(
        functools.partial(_gla_kernel, chunk=chunk, nchunks=tile // chunk, nt=nt, heads=h, hk=hk, hv=hv,
                          has_s0=s0_all is not None),
        grid=(b, nt),
        in_specs=in_specs,
        out_specs=[vspec, sspec],
        out_shape=[jax.ShapeDtypeStruct((m, h * hv), BF16), jax.ShapeDtypeStruct(s_all.shape, F32)],
        scratch_shapes=[pltpu.VMEM((h, hv, hk), F32)],
        input_output_aliases={len(args) - 1: 1},
        compiler_params=_cparams(("parallel", "arbitrary")),
        name="gla",
    )(*args)


def _lambda(lam_ref, lam_init):
    lf = lam_ref[...]
    a = jnp.sum(lf[0:1] * lf[1:2], axis=-1, keepdims=True)
    b = jnp.sum(lf[2:3] * lf[3:4], axis=-1, keepdims=True)
    return jnp.exp(a) - jnp.exp(b) + lam_init


def _attn_prompt_kernel(q_ref, kt_ref, v_ref, ng_ref, lam_ref, o_ref, qm_scr, vx_scr, m_scr, acc_scr,
                        *, tq, tk, hd, lam_init):
    qi = pl.program_id(2)
    w = 2 * hd

    @pl.when(qi == 0)
    def _():
        vx_scr[:, 0:w] = v_ref[...]
        vx_scr[:, w:2 * w] = jnp.ones((v_ref.shape[0], w), BF16)

    q = q_ref[...]
    lane = lax.broadcasted_iota(jnp.int32, q.shape, 1)
    zero = jnp.zeros_like(q)
    qm_scr[0] = jnp.where(lane < hd, q, zero)
    qm_scr[1] = jnp.where(lane >= hd, q, zero)
    m_scr[...] = jnp.full(m_scr.shape, -jnp.inf, F32)
    acc_scr[...] = jnp.zeros(acc_scr.shape, F32)

    def step(j, row_lo):
        cols = pl.ds(pl.multiple_of(j * tk, tk), tk)
        kb = kt_ref[:, cols]
        vb = vx_scr[cols, :]
        rows = slice(0 if row_lo is None else row_lo, tq)
        for c in range(2):
            s = _dot(qm_scr[c, rows], kb)
            if row_lo is not None:
                r_i = lax.broadcasted_iota(jnp.int32, s.shape, 0)
                c_i = lax.broadcasted_iota(jnp.int32, s.shape, 1)
                s = jnp.where(r_i >= c_i, s, -jnp.inf)
            m_prev = m_scr[c, rows]
            m_new = jnp.maximum(m_prev, jnp.max(s, axis=-1, keepdims=True))
            alpha = jnp.exp2(m_prev - m_new)
            p = jnp.exp2(s - jnp.tile(m_new, (1, tk // LANES)))
            acc_scr[c, rows] = jnp.tile(alpha, (1, 2)) * acc_scr[c, rows] + _dot(p.astype(BF16), vb)
            m_scr[c, rows] = m_new

    def body(j, carry):
        step(j, None)
        return carry

    nd = tq // tk
    lax.fori_loop(0, qi * nd, body, 0)
    for dblk in range(nd):
        step(qi * nd + dblk, dblk * tk)

    lam = _lambda(lam_ref, lam_init)
    a0 = acc_scr[0]
    a1 = acc_scr[1]
    o = a0[:, 0:w] / a0[:, w:2 * w] - lam * (a1[:, 0:w] / a1[:, w:2 * w])
    o_ref[...] = (_rms_norm(o, ng_ref[...]) * (1.0 - lam_init)).astype(BF16)


def _attn_prompt(q, kt, v, ng, lam_p, *, batch, heads, hd, seq, tq, tk, lam_init):
    m, w = q.shape
    nq = seq // tq
    assert 2 * hd == LANES and tq % tk == 0
    qspec = pl.BlockSpec((tq, 2 * hd), lambda b, h, i: (b * nq + i, h))
    return pl.pallas_call(
        functools.partial(_attn_prompt_kernel, tq=tq, tk=tk, hd=hd, lam_init=lam_init),
        grid=(batch, heads, nq),
        in_specs=[qspec,
                  pl.BlockSpec((None, 2 * hd, seq), lambda b, h, i: (b, h, 0)),
                  pl.BlockSpec((seq, 2 * hd), lambda b, h, i: (b, h)),
                  pl.BlockSpec((1, 2 * hd), lambda b, h, i: (0, 0)),
                  pl.BlockSpec((4, hd), lambda b, h, i: (0, 0))],
        out_specs=qspec,
        out_shape=jax.ShapeDtypeStruct((m, w), BF16),
        scratch_shapes=[pltpu.VMEM((2, tq, 2 * hd), BF16), pltpu.VMEM((seq, 4 * hd), BF16),
                        pltpu.VMEM((2, tq, 2 * hd), F32), pltpu.VMEM((2, tq, 4 * hd), F32)],
        compiler_params=_cparams(("parallel", "parallel", "arbitrary")),
        name="attn_prompt",
    )(q, kt, v, ng, lam_p)


def _attn_decode_kernel(pt_ref, qbd_ref, kn_ref, vn_ref, ng_ref, lam_ref, *refs, pg, dec_seq, heads, hd, lam_init):
    del pt_ref
    k_refs = refs[:pg]
    v_refs = refs[pg:2 * pg]
    o_ref, m_scr, l_scr, acc_scr = refs[2 * pg:]
    j = pl.program_id(1)
    nj = pl.num_programs(1)
    qbd = qbd_ref[...]
    grp = 2 * dec_seq
    page = kn_ref.shape[-1]

    def pv(p, v_ref):
        outs = []
        for h in range(heads):
            v_h = v_ref[pl.ds(h, page, stride=heads), :].astype(BF16)
            outs.append(_dot(p[h * grp:(h + 1) * grp], v_h))
        return jnp.concatenate(outs, axis=0)

    @pl.when(j == 0)
    def _():
        s = _dot(qbd, kn_ref[...])
        r_i = lax.broadcasted_iota(jnp.int32, s.shape, 0) % dec_seq
        c_i = lax.broadcasted_iota(jnp.int32, s.shape, 1)
        s = jnp.where(c_i <= r_i, s, -jnp.inf)
        m0 = jnp.max(s, axis=-1, keepdims=True)
        p = jnp.exp2(s - m0)
        m_scr[...] = jnp.broadcast_to(m0, m_scr.shape)
        l_scr[...] = jnp.broadcast_to(jnp.sum(p, axis=-1, keepdims=True), l_scr.shape)
        acc_scr[...] = pv(p.astype(BF16), vn_ref)

    s = jnp.concatenate([_dot(qbd, kr[...].astype(BF16)) for kr in k_refs], axis=-1)
    m_prev = m_scr[...]
    m_new = jnp.maximum(m_prev, jnp.max(s, axis=-1, keepdims=True))
    alpha = jnp.exp2(m_prev - m_new)
    p = jnp.exp2(s - jnp.tile(m_new, (1, s.shape[1] // LANES)))
    l_scr[...] = alpha * l_scr[...] + jnp.sum(p, axis=-1, keepdims=True)
    pb = p.astype(BF16)
    upd = pv(pb[:, 0:page], v_refs[0])
    for i in range(1, pg):
        upd = upd + pv(pb[:, i * page:(i + 1) * page], v_refs[i])
    acc_scr[...] = alpha * acc_scr[...] + upd
    m_scr[...] = m_new

    @pl.when(j == nj - 1)
    def _():
        lam = _lambda(lam_ref, lam_init)
        ng = ng_ref[...]
        n = acc_scr[...] / l_scr[...]
        for h in range(heads):
            blk = n[h * grp:(h + 1) * grp]
            o = blk[0:dec_seq] - lam * blk[dec_seq:grp]
            o_ref[:, h * 2 * hd:(h + 1) * 2 * hd] = _rms_norm(o, ng) * (1.0 - lam_init)


def _attn_decode(page_table, qbd, kt_new, v_new, ng, lam_p, cache_kt, cache_v2, layer, *, pg, dec_seq, heads, hd,
                 lam_init):
    db, r, w = qbd.shape
    n_pages = page_table.shape[1]
    page = cache_kt.shape[-1]
    assert n_pages % pg == 0 and page == LANES and 2 * hd == LANES

    def kspec(i):
        return pl.BlockSpec((None, None, w, page), lambda b, j, pt, i=i: (layer, pt[b, j * pg + i], 0, 0))

    def vspec(i):
        return pl.BlockSpec((None, None, page * heads, 2 * hd),
                            lambda b, j, pt, i=i: (layer, pt[b, j * pg + i], 0, 0))

    in_specs = [pl.BlockSpec((None, r, w), lambda b, j, pt: (b, 0, 0)),
                pl.BlockSpec((None, w, page), lambda b, j, pt: (b, 0, 0)),
                pl.BlockSpec((None, page * heads, 2 * hd), lambda b, j, pt: (b, 0, 0)),
                pl.BlockSpec((1, 2 * hd), lambda b, j, pt: (0, 0)),
                pl.BlockSpec((4, hd), lambda b, j, pt: (0, 0))]
    in_specs += [kspec(i) for i in range(pg)] + [vspec(i) for i in range(pg)]
    return pl.pallas_call(
        functools.partial(_attn_decode_kernel, pg=pg, dec_seq=dec_seq, heads=heads, hd=hd, lam_init=lam_init),
        grid_spec=pltpu.PrefetchScalarGridSpec(
            num_scalar_prefetch=1,
            grid=(db, n_pages // pg),
            in_specs=in_specs,
            out_specs=pl.BlockSpec((None, dec_seq, w), lambda b, j, pt: (b, 0, 0)),
            scratch_shapes=[pltpu.VMEM((r, 2 * hd), F32)] * 3),
        out_shape=jax.ShapeDtypeStruct((db, dec_seq, w), F32),
        compiler_params=_cparams(("parallel", "arbitrary")),
        name="attn_decode",
    )(page_table, qbd, kt_new, v_new, ng, lam_p, *([cache_kt] * pg), *([cache_v2] * pg))


def _mix_kernel(x_ref, sh_ref, sc_ref, gt_ref, oa_ref, ob_ref, wga_ref, wgb_ref, wa_ref, wb_ref, wo_ref,
                g_ref, b_ref, o_ref, *, alpha):
    x = x_ref[...]
    u = (x * (1.0 + sc_ref[...]) + sh_ref[...]).astype(BF16)
    ya = _sigmoid(_dot(u, wga_ref[...])) * _dot(oa_ref[...], wa_ref[...])
    yb = _sigmoid(_dot(u, wgb_ref[...])) * _dot(ob_ref[...], wb_ref[...])
    z = _dot((ya + yb).astype(BF16), wo_ref[...])
    o_ref[...] = _layer_norm(alpha * x + gt_ref[...] * z, g_ref[...], b_ref[...])


def _mix(x, mod, layer, oa, ob, wgate, wa, wb, wo, ln_g, ln_b, *, alpha, tm, rows_per_batch):
    m, d = x.shape
    row = pl.BlockSpec((tm, d), lambda i: (i, 0))
    wsq = pl.BlockSpec((None, d, d), lambda i: (layer, 0, 0))
    in_specs = [row] + _mod_specs(mod, layer, 1, tm, rows_per_batch, (0, 1, 2)) + [
        row, row,
        pl.BlockSpec((None, d, d), lambda i: (layer, 0, 0)),
        pl.BlockSpec((None, d, d), lambda i: (layer, 0, 1)),
        wsq, wsq, wsq,
        pl.BlockSpec((None, None, 1, d), lambda i: (layer, 1, 0, 0)),
        pl.BlockSpec((None, None, 1, d), lambda i: (layer, 1, 0, 0)),
    ]
    return pl.pallas_call(
        functools.partial(_mix_kernel, alpha=alpha),
        grid=(m // tm,),
        in_specs=in_specs,
        out_specs=row,
        out_shape=jax.ShapeDtypeStruct((m, d), F32),
        compiler_params=_cparams(("parallel",)),
        name="mix",
    )(x, mod, mod, mod, oa, ob, wgate, wgate, wa, wb, wo, ln_g, ln_b)


def _rope_angles(pos, hd):
    half = hd // 2
    inv = ROPE_THETA ** (-jnp.arange(half, dtype=F32) * (2.0 / hd))
    ang = pos.astype(F32)[:, None] * inv[None, :]
    return jnp.cos(ang), jnp.sin(ang)


def _rope_lane_tables(cos, sin, hd):
    reps = LANES // hd
    return (jnp.tile(jnp.concatenate([cos, cos], -1), (1, reps)),
            jnp.tile(jnp.concatenate([-sin, sin], -1), (1, reps)))


def _tile(n, pref):
    t = min(n, pref)
    while n % t:
        t //= 2
    return t


def _ffn_tile(f):
    best = LANES
    for t in range(LANES, min(f, 1536) + 1, LANES):
        if f % t == 0:
            best = t
    return best


def kernel(x_prompt, x_sample, cache_k, cache_v, state_gla, page_table, c_prompt, c_sample, w_ada, b_ada,
           ln_g, ln_b, ffn_w_in, ffn_w_out, w_in, gla_w_f2, gla_b_f2, gla_norm_g, diff_lam, diff_norm_g,
           w_branch_a, w_branch_b, w_out):
    batch, seq, d = x_prompt.shape
    db, dec_seq, _ = x_sample.shape
    depth = w_ada.shape[0]
    gla_heads, hk, hv = state_gla.shape[2:]
    n_pool, page, heads, _, hd = cache_k.shape[1:]
    dk, dv = gla_heads * hk, gla_heads * hv
    wq = 2 * heads * hd
    rank = gla_w_f2.shape[1]
    f = ffn_w_out.shape[2]
    past = page_table.shape[1] * page
    alpha = (2.0 * depth) ** 0.25
    mp, ms = batch * seq, db * dec_seq

    o_f = 2 * dk + 2 * dv
    o_dq = o_f + rank
    o_ga = o_dq + 3 * wq
    wg = w_in[:, :, 0:o_f].astype(BF16)
    rank_pad = -(-rank // LANES) * LANES
    wf1 = jnp.pad(w_in[:, :, o_f:o_dq], ((0, 0), (0, 0), (0, rank_pad - rank))).astype(BF16)
    wf2 = jnp.pad(gla_w_f2, ((0, 0), (0, rank_pad - rank), (0, 0))).astype(BF16)
    bf2 = gla_b_f2.reshape(depth, 1, dk)
    wd = w_in[:, :, o_dq:o_ga].astype(BF16)
    wkt = jnp.swapaxes(w_in[:, :, o_dq + wq:o_dq + 2 * wq], 1, 2).astype(BF16)
    wgate = w_in[:, :, o_ga:o_ga + 2 * d].astype(BF16)
    ffn_in = ffn_w_in.astype(BF16)
    ffn_out = ffn_w_out.astype(BF16)
    wa = w_branch_a.astype(BF16)
    wb = w_branch_b.astype(BF16)
    wo = w_out.astype(BF16)
    ln_g4 = ln_g.reshape(depth, 3, 1, d)
    ln_b4 = ln_b.reshape(depth, 3, 1, d)
    gla_ng = gla_norm_g.reshape(depth, 1, hv)
    diff_ng = diff_norm_g.reshape(depth, 1, 2 * hd)

    n_c = batch + db
    n_c_pad = -(-n_c // 8) * 8
    c_all = jnp.pad(jnp.concatenate([c_prompt, c_sample], 0), ((0, n_c_pad - n_c), (0, 0)))
    mod = _ada_mod(c_all, w_ada, b_ada)
    mod_p = mod[:, :batch].reshape(depth, batch, 1, N_MOD * d)
    mod_s = jnp.repeat(mod[:, batch:n_c], dec_seq, axis=1)

    cos_p, sin_p = _rope_angles(jnp.arange(seq, dtype=jnp.int32), hd)
    cos_pl, sin_pl = _rope_lane_tables(cos_p, sin_p, hd)
    cos_pt, sin_pt = cos_p.T, sin_p.T
    cos_s, sin_s = _rope_lane_tables(*_rope_angles(past + jnp.arange(dec_seq, dtype=jnp.int32), hd), hd)
    cos_s = jnp.tile(cos_s, (db, 1))
    sin_s = jnp.tile(sin_s, (db, 1))

    tm_p = _tile(seq, 512)
    tf = _ffn_tile(f)
    tq = _tile(seq, 1024)
    tk = _tile(tq, 512)
    gla_tile = _tile(seq, 512)
    gla_chunk = _tile(gla_tile, 64)
    dec_pad = -(-dec_seq // GLA_SUB) * GLA_SUB
    pg = _tile(page_table.shape[1], 8)
    cache_kt = cache_k.transpose(0, 1, 3, 4, 5, 2).reshape(depth, n_pool, wq, page)
    cache_v2 = cache_v.reshape(depth, n_pool, page * heads, 2 * hd)
    eye_hc = jnp.eye(2 * heads, dtype=BF16)

    hp = x_prompt.reshape(mp, d)
    hs = x_sample.reshape(ms, d)
    k_all = jnp.zeros((depth, batch, wq, seq), F32)
    v_all = jnp.zeros((depth, mp * heads, 2 * hd), F32)
    sp_all = jnp.zeros((depth, batch, gla_heads, hk, hv), F32)
    sd_all = jnp.zeros((depth, db, gla_heads, hk, hv), F32)
    kd, vd = [], []

    def pad_rows(t):
        t3 = t.reshape(db, dec_seq, t.shape[-1])
        return jnp.pad(t3, ((0, 0), (0, dec_pad - dec_seq), (0, 0))).reshape(db * dec_pad, t.shape[-1])

    for l in range(depth):
        lam_init = 0.8 - 0.6 * math.exp(-0.3 * l)

        hp = _ffn(hp, mod_p, l, 0, 0, ffn_in, ffn_out, ln_g4, ln_b4, alpha=alpha, tm=tm_p, tf=tf,
                  rows_per_batch=seq)
        gq, gk, gv, gr, gg = _proj_gla(hp, mod_p, l, wg, wf1, wf2, bf2, dk=dk, dv=dv, hk=hk, tm=tm_p,
                                       rows_per_batch=seq)
        k_all, v_all, dq, ktb, vb = _proj_diff_cols(hp, mod_p, l, wd, wkt, cos_pl, sin_pl, cos_pt, sin_pt,
                                                    k_all, v_all, hd=hd, heads=heads, tm=tm_p, seq=seq)
        oa, sp_all = _gla(gq, gk, gg, gv, gr, gla_ng[l], sp_all, l, rows_per_batch=seq, tile=gla_tile,
                          chunk=gla_chunk)
        ob = _attn_prompt(dq, ktb, vb, diff_ng[l], diff_lam[l], batch=batch, heads=heads, hd=hd, seq=seq,
                          tq=tq, tk=tk, lam_init=lam_init)
        hp = _mix(hp, mod_p, l, oa, ob, wgate, wa, wb, wo, ln_g4, ln_b4, alpha=alpha, tm=tm_p,
                  rows_per_batch=seq)
        hp = _ffn(hp, mod_p, l, 2, 1, ffn_in, ffn_out, ln_g4, ln_b4, alpha=alpha, tm=tm_p, tf=tf,
                  rows_per_batch=seq)

        hs = _ffn(hs, mod_s, l, 0, 0, ffn_in, ffn_out, ln_g4, ln_b4, alpha=alpha, tm=ms, tf=tf,
                  rows_per_batch=None)
        gq, gk, gv, gr, gg = _proj_gla(hs, mod_s, l, wg, wf1, wf2, bf2, dk=dk, dv=dv, hk=hk, tm=ms,
                                       rows_per_batch=None)
        dq, kf, kb, vf = _proj_diff_rows(hs, mod_s, l, wd, cos_s, sin_s, hd=hd, tm=ms)
        oa, sd_all = _gla(pad_rows(gq), pad_rows(gk), pad_rows(gg), pad_rows(gv), pad_rows(gr), gla_ng[l],
                          sd_all, l, state_gla, rows_per_batch=dec_pad, tile=dec_pad, chunk=dec_pad)
        oa = oa.reshape(db, dec_pad, dv)[:, :dec_seq].reshape(ms, dv)
        q5 = dq.reshape(db, dec_seq, 2 * heads, hd)
        qbd = (q5.transpose(0, 2, 1, 3)[:, :, :, None, :] * eye_hc[None, :, None, :, None]).reshape(
            db, 2 * heads * dec_seq, wq)
        kt_new = jnp.pad(kb.reshape(db, dec_seq, wq).transpose(0, 2, 1), ((0, 0), (0, 0), (0, page - dec_seq)))
        v_new = jnp.pad(vf.reshape(db, dec_seq, wq), ((0, 0), (0, page - dec_seq), (0, 0))).reshape(
            db, page * heads, 2 * hd)
        ob = _attn_decode(page_table, qbd, kt_new, v_new, diff_ng[l], diff_lam[l], cache_kt, cache_v2, l, pg=pg,
                          dec_seq=dec_seq, heads=heads, hd=hd, lam_init=lam_init)
        ob = ob.reshape(ms, wq).astype(BF16)
        hs = _mix(hs, mod_s, l, oa, ob, wgate, wa, wb, wo, ln_g4, ln_b4, alpha=alpha, tm=ms,
                  rows_per_batch=None)
        hs = _ffn(hs, mod_s, l, 2, 1, ffn_in, ffn_out, ln_g4, ln_b4, alpha=alpha, tm=ms, tf=tf,
                  rows_per_batch=None)
        kd.append(kf)
        vd.append(vf)

    k_prompt = k_all.reshape(depth, batch, heads, 2, hd, seq).transpose(0, 1, 5, 2, 3, 4)
    return (hp.reshape(batch, seq, d),
            hs.reshape(db, dec_seq, d),
            k_prompt,
            v_all.reshape(depth, batch, seq, heads, 2 * hd),
            sp_all,
            jnp.stack(kd).reshape(depth, db, dec_seq, heads, 2, hd),
            jnp.stack(vd).reshape(depth, db, dec_seq, heads, 2 * hd),
            sd_all)
```

```python
import functools
import math

import jax
import jax.numpy as jnp
from jax import lax
from jax.experimental import pallas as pl
from jax.experimental.pallas import tpu as pltpu

F32 = jnp.float32
BF16 = jnp.bfloat16

LN_EPS = 1e-5
GLA_TAU = 16.0
ROPE_THETA = 10000.0
N_MOD = 9
LANES = 128
GLA_SUB = 8
BF16_ROWS = 16
VMEM_LIMIT_BYTES = 56 * 1024 * 1024
LOG2E = math.log2(math.e)


def _cparams(sem):
    return pltpu.CompilerParams(dimension_semantics=sem, vmem_limit_bytes=VMEM_LIMIT_BYTES)


def _dot(a, b):
    return jnp.dot(a, b, preferred_element_type=F32)


def _dot_nt(a, b):
    return lax.dot_general(a, b, (((1,), (1,)), ((), ())), preferred_element_type=F32)


def _dot_tn(a, b):
    return lax.dot_general(a, b, (((0,), (0,)), ((), ())), preferred_element_type=F32)


def _sigmoid(x):
    return 1.0 / (1.0 + jnp.exp(-x))


def _silu(x):
    return x * _sigmoid(x)


def _layer_norm(y, g, b):
    mu = jnp.mean(y, axis=-1, keepdims=True)
    yc = y - mu
    var = jnp.mean(yc * yc, axis=-1, keepdims=True)
    return yc * lax.rsqrt(var + LN_EPS) * g + b


def _rms_norm(o, g):
    return o * lax.rsqrt(jnp.mean(o * o, axis=-1, keepdims=True) + LN_EPS) * g


def _modulated(x_ref, sh_ref, sc_ref):
    return (x_ref[...] * (1.0 + sc_ref[...]) + sh_ref[...]).astype(BF16)


def _mod_specs(mod, layer, sub, tm, rows_per_batch, which):
    d = mod.shape[-1] // N_MOD
    specs = []
    for w in which:
        col = 3 * sub + w
        if rows_per_batch is not None:
            tiles = rows_per_batch // tm
            specs.append(pl.BlockSpec((None, None, 1, d),
                                      lambda i, *_, col=col, tiles=tiles: (layer, i // tiles, 0, col)))
        else:
            specs.append(pl.BlockSpec((None, tm, d), lambda i, *_, col=col: (layer, i, col)))
    return specs


def _ada_kernel(c_ref, w_ref, b_ref, o_ref):
    a = _silu(c_ref[...]).astype(BF16)
    o_ref[...] = _dot(a, w_ref[...].astype(BF16)) + b_ref[...]


def _ada_mod(c_all, w_ada, b_ada):
    depth, d, n = w_ada.shape
    r = c_all.shape[0]
    tn = d
    return pl.pallas_call(
        _ada_kernel,
        grid=(depth, n // tn),
        in_specs=[pl.BlockSpec((r, d), lambda l, j: (0, 0)),
                  pl.BlockSpec((None, d, tn), lambda l, j: (l, 0, j)),
                  pl.BlockSpec((None, 1, tn), lambda l, j: (l, 0, j))],
        out_specs=pl.BlockSpec((None, r, tn), lambda l, j: (l, 0, j)),
        out_shape=jax.ShapeDtypeStruct((depth, r, n), F32),
        compiler_params=_cparams(("parallel", "parallel")),
        name="ada_mod",
    )(c_all, w_ada, b_ada.reshape(depth, 1, n))


def _ffn_kernel(x_ref, sh_ref, sc_ref, gt_ref, wi_ref, wo_ref, g_ref, b_ref, o_ref, *, alpha, f, fc):
    x = x_ref[...]
    u = (x * (1.0 + sc_ref[...]) + sh_ref[...]).astype(BF16)

    def up(k):
        return (_dot(u, wi_ref[:, k * fc:(k + 1) * fc]), _dot(u, wi_ref[:, f + k * fc:f + (k + 1) * fc]))

    nk = f // fc
    cur = up(0)
    acc = None
    for k in range(nk):
        nxt = up(k + 1) if k + 1 < nk else None
        part = _dot((_silu(cur[0]) * cur[1]).astype(BF16), wo_ref[k * fc:(k + 1) * fc, :])
        acc = part if acc is None else acc + part
        cur = nxt
    y = alpha * x + gt_ref[...] * (0.5 * acc)
    o_ref[...] = _layer_norm(y, g_ref[...], b_ref[...])


def _resident(block_shape, index_map):
    return pl.BlockSpec(block_shape, index_map, pipeline_mode=pl.Buffered(1))


def _ffn(x, mod, layer, sub, which_ffn, w_in, w_out, ln_g, ln_b, *, alpha, tm, fc, rows_per_batch):
    m, d = x.shape
    f = w_out.shape[2]
    assert f % fc == 0
    in_specs = [pl.BlockSpec((tm, d), lambda i: (i, 0))]
    in_specs += _mod_specs(mod, layer, sub, tm, rows_per_batch, (0, 1, 2))
    in_specs += [
        _resident((None, None, d, 2 * f), lambda i: (layer, which_ffn, 0, 0)),
        _resident((None, None, f, d), lambda i: (layer, which_ffn, 0, 0)),
        pl.BlockSpec((None, None, 1, d), lambda i: (layer, sub, 0, 0)),
        pl.BlockSpec((None, None, 1, d), lambda i: (layer, sub, 0, 0)),
    ]
    return pl.pallas_call(
        functools.partial(_ffn_kernel, alpha=alpha, f=f, fc=fc),
        grid=(m // tm,),
        in_specs=in_specs,
        out_specs=pl.BlockSpec((tm, d), lambda i: (i, 0)),
        out_shape=jax.ShapeDtypeStruct((m, d), F32),
        compiler_params=_cparams(("parallel",)),
        name="ffn",
    )(x, mod, mod, mod, w_in, w_out, ln_g, ln_b)


def _proj_gla_kernel(x_ref, sh_ref, sc_ref, wg_ref, wf1_ref, wf2_ref, bf2_ref,
                     q_ref, k_ref, v_ref, r_ref, g_ref, *, dk, dv, qscale):
    u = _modulated(x_ref, sh_ref, sc_ref)
    q_ref[...] = _dot(u, wg_ref[:, 0:dk]) * qscale
    k_ref[...] = _dot(u, wg_ref[:, dk:2 * dk])
    v_ref[...] = _dot(u, wg_ref[:, 2 * dk:2 * dk + dv]).astype(BF16)
    r_ref[...] = _silu(_dot(u, wg_ref[:, 2 * dk + dv:2 * dk + 2 * dv]))
    gf = _dot(u, wf1_ref[...])
    z = _dot(gf.astype(BF16), wf2_ref[...]) + bf2_ref[...]
    g_ref[...] = (jnp.minimum(z, 0.0) - jnp.log1p(jnp.exp(-jnp.abs(z)))) * (1.0 / GLA_TAU)


def _proj_gla(x, mod, layer, wg, wf1, wf2, bf2, *, dk, dv, hk, tm, rows_per_batch):
    m, d = x.shape
    rp = wf1.shape[-1]
    in_specs = [pl.BlockSpec((tm, d), lambda i: (i, 0))]
    in_specs += _mod_specs(mod, layer, 1, tm, rows_per_batch, (0, 1))
    in_specs += [
        pl.BlockSpec((None, d, 2 * dk + 2 * dv), lambda i: (layer, 0, 0)),
        pl.BlockSpec((None, d, rp), lambda i: (layer, 0, 0)),
        pl.BlockSpec((None, rp, dk), lambda i: (layer, 0, 0)),
        pl.BlockSpec((None, 1, dk), lambda i: (layer, 0, 0)),
    ]
    out_shapes = [jax.ShapeDtypeStruct((m, dk), F32), jax.ShapeDtypeStruct((m, dk), F32),
                  jax.ShapeDtypeStruct((m, dv), BF16), jax.ShapeDtypeStruct((m, dv), F32),
                  jax.ShapeDtypeStruct((m, dk), F32)]
    return pl.pallas_call(
        functools.partial(_proj_gla_kernel, dk=dk, dv=dv, qscale=hk ** -0.5),
        grid=(m // tm,),
        in_specs=in_specs,
        out_specs=[pl.BlockSpec((tm, s.shape[1]), lambda i: (i, 0)) for s in out_shapes],
        out_shape=out_shapes,
        compiler_params=_cparams(("parallel",)),
        name="proj_gla",
    )(x, mod, mod, wg, wf1, wf2, bf2)


def _rope_lanes(t, cos, sin, first_half, hd):
    rot = jnp.where(first_half, pltpu.roll(t, LANES - hd // 2, 1), pltpu.roll(t, hd // 2, 1))
    return t * cos + rot * sin


def _proj_q(u, wd_ref, cos, sin, first_half, q_ref, w, hd, qscale):
    dq = _dot(u, wd_ref[:, 0:w])
    for c in range(w // LANES):
        sl = slice(c * LANES, (c + 1) * LANES)
        q_ref[:, sl] = (_rope_lanes(dq[:, sl], cos, sin, first_half, hd) * qscale).astype(BF16)


def _proj_diff_rows_kernel(x_ref, sh_ref, sc_ref, wd_ref, cos_ref, sin_ref,
                           q_ref, kf_ref, kb_ref, vf_ref, *, w, hd, qscale):
    u = _modulated(x_ref, sh_ref, sc_ref)
    cos = cos_ref[...]
    sin = sin_ref[...]
    first_half = (lax.broadcasted_iota(jnp.int32, cos.shape, 1) % hd) < (hd // 2)
    _proj_q(u, wd_ref, cos, sin, first_half, q_ref, w, hd, qscale)
    dk = _dot(u, wd_ref[:, w:2 * w])
    for c in range(w // LANES):
        sl = slice(c * LANES, (c + 1) * LANES)
        kr = _rope_lanes(dk[:, sl], cos, sin, first_half, hd)
        kf_ref[:, sl] = kr
        kb_ref[:, sl] = kr.astype(BF16)
    vf_ref[...] = _dot(u, wd_ref[:, 2 * w:3 * w])


def _proj_diff_cols_kernel(x_ref, sh_ref, sc_ref, wd_ref, wkt_ref, cos_ref, sin_ref, cost_ref, sint_ref,
                           k_all_ref, v_all_ref, kt_ref, vf_ref, q_ref, ktb_ref, vb_ref,
                           *, w, hd, heads, qscale):
    del k_all_ref, v_all_ref
    u = _modulated(x_ref, sh_ref, sc_ref)
    cos = cos_ref[...]
    sin = sin_ref[...]
    first_half = (lax.broadcasted_iota(jnp.int32, cos.shape, 1) % hd) < (hd // 2)
    _proj_q(u, wd_ref, cos, sin, first_half, q_ref, w, hd, qscale)
    dkt = _dot_nt(wkt_ref[...], u)
    cost = cost_ref[...]
    sint = sint_ref[...]
    half = hd // 2
    for g in range(w // hd):
        x1 = dkt[g * hd:g * hd + half]
        x2 = dkt[g * hd + half:(g + 1) * hd]
        r1 = x1 * cost - x2 * sint
        r2 = x2 * cost + x1 * sint
        kt_ref[g * hd:g * hd + half, :] = r1
        kt_ref[g * hd + half:(g + 1) * hd, :] = r2
        ktb_ref[g * hd:g * hd + half, :] = r1.astype(BF16)
        ktb_ref[g * hd + half:(g + 1) * hd, :] = r2.astype(BF16)
    dv = _dot(u, wd_ref[:, 2 * w:3 * w])
    tm = dv.shape[0]
    for h in range(heads):
        vf_ref[pl.ds(h, tm, stride=heads), :] = dv[:, h * 2 * hd:(h + 1) * 2 * hd]
    vb_ref[...] = dv.astype(BF16)


def _proj_diff_rows(x, mod, layer, wd, cos_t, sin_t, *, hd, tm):
    m, d = x.shape
    w = wd.shape[-1] // 3
    in_specs = [pl.BlockSpec((tm, d), lambda i: (i, 0))]
    in_specs += _mod_specs(mod, layer, 1, tm, None, (0, 1))
    in_specs += [
        pl.BlockSpec((None, d, 3 * w), lambda i: (layer, 0, 0)),
        pl.BlockSpec((tm, LANES), lambda i: (i, 0)),
        pl.BlockSpec((tm, LANES), lambda i: (i, 0)),
    ]
    out_shapes = [jax.ShapeDtypeStruct((m, w), BF16), jax.ShapeDtypeStruct((m, w), F32),
                  jax.ShapeDtypeStruct((m, w), BF16), jax.ShapeDtypeStruct((m, w), F32)]
    return pl.pallas_call(
        functools.partial(_proj_diff_rows_kernel, w=w, hd=hd, qscale=hd ** -0.5 * LOG2E),
        grid=(m // tm,),
        in_specs=in_specs,
        out_specs=[pl.BlockSpec((tm, w), lambda i: (i, 0)) for _ in out_shapes],
        out_shape=out_shapes,
        compiler_params=_cparams(("parallel",)),
        name="proj_diff_rows",
    )(x, mod, mod, wd, cos_t, sin_t)


def _proj_diff_cols(x, mod, layer, wd, wkt, cos_t, sin_t, cos_tt, sin_tt, k_all, v_all, *, hd, heads, tm, seq):
    m, d = x.shape
    w = wkt.shape[1]
    npos = seq // tm
    batch = m // seq
    in_specs = [pl.BlockSpec((tm, d), lambda i: (i, 0))]
    in_specs += _mod_specs(mod, layer, 1, tm, seq, (0, 1))
    in_specs += [
        pl.BlockSpec((None, d, 3 * w), lambda i: (layer, 0, 0)),
        pl.BlockSpec((None, w, d), lambda i: (layer, 0, 0)),
        pl.BlockSpec((tm, LANES), lambda i: (i % npos, 0)),
        pl.BlockSpec((tm, LANES), lambda i: (i % npos, 0)),
        pl.BlockSpec((hd // 2, tm), lambda i: (0, i % npos)),
        pl.BlockSpec((hd // 2, tm), lambda i: (0, i % npos)),
        pl.BlockSpec(memory_space=pl.ANY),
        pl.BlockSpec(memory_space=pl.ANY),
    ]
    out_shapes = [jax.ShapeDtypeStruct(k_all.shape, F32), jax.ShapeDtypeStruct(v_all.shape, F32),
                  jax.ShapeDtypeStruct((m, w), BF16),
                  jax.ShapeDtypeStruct((batch, w, seq), BF16), jax.ShapeDtypeStruct((m, w), BF16)]
    out_specs = [
        pl.BlockSpec((None, None, w, tm), lambda i: (layer, i // npos, 0, i % npos)),
        pl.BlockSpec((None, tm * heads, 2 * hd), lambda i: (layer, i, 0)),
        pl.BlockSpec((tm, w), lambda i: (i, 0)),
        pl.BlockSpec((None, w, tm), lambda i: (i // npos, 0, i % npos)),
        pl.BlockSpec((tm, w), lambda i: (i, 0)),
    ]
    return pl.pallas_call(
        functools.partial(_proj_diff_cols_kernel, w=w, hd=hd, heads=heads, qscale=hd ** -0.5 * LOG2E),
        grid=(m // tm,),
        in_specs=in_specs,
        out_specs=out_specs,
        out_shape=out_shapes,
        input_output_aliases={9: 0, 10: 1},
        compiler_params=_cparams(("parallel",)),
        name="proj_diff_cols",
    )(x, mod, mod, wd, wkt, cos_t, sin_t, cos_tt, sin_tt, k_all, v_all)


def _gla_chunk(qs, ks, g_all, vs, rs, sts, ng, tril, causal, sub_lane):
    heads = len(qs)
    c_rows, hk = qs[0].shape
    nsub = c_rows // GLA_SUB
    cum_all = jnp.dot(tril, g_all * LOG2E, precision=lax.Precision.HIGHEST, preferred_element_type=F32)
    cums = [cum_all[:, h * hk:(h + 1) * hk] for h in range(heads)]
    lasts = [cum[c_rows - 1:c_rows] for cum in cums]

    inter = [_dot_nt((q * jnp.exp2(cum)).astype(BF16), st.astype(BF16)) for q, cum, st in zip(qs, cums, sts)]
    offs = []
    for q, k, cum in zip(qs, ks, cums):
        per_block = [None]
        for i in range(1, nsub):
            lo = i * GLA_SUB
            base = cum[lo - 1:lo]
            qt = (q[lo:lo + GLA_SUB] * jnp.exp2(cum[lo:lo + GLA_SUB] - base)).astype(BF16)
            kt = jnp.concatenate([(k[0:lo] * jnp.exp2(base - cum[0:lo])).astype(F32),
                                  jnp.zeros((c_rows - lo, hk), F32)], axis=0).astype(BF16)
            per_block.append(_dot_nt(qt, kt))
        offs.append(per_block)
    new_sts = [st * jnp.exp2(last) + _dot_tn(v, (k * jnp.exp2(last - cum)).astype(BF16))
               for k, v, cum, last, st in zip(ks, vs, cums, lasts, sts)]

    atts = []
    for q, k, cum, off in zip(qs, ks, cums, offs):
        blocks = []
        for i in range(nsub):
            lo = i * GLA_SUB
            qi = q[lo:lo + GLA_SUB]
            ci = cum[lo:lo + GLA_SUB]
            acc = jnp.zeros((GLA_SUB, c_rows), F32)
            for s in range(GLA_SUB):
                dec = jnp.exp2(jnp.minimum(ci - ci[s:s + 1], 0.0))
                a_col = jnp.sum(qi * k[lo + s:lo + s + 1] * dec, axis=-1, keepdims=True)
                acc = jnp.where(sub_lane == lo + s, a_col, acc)
            if i > 0:
                acc = jnp.where(sub_lane < lo, off[i], acc)
            blocks.append(acc)
        att = blocks[0] if nsub == 1 else jnp.concatenate(blocks, axis=0)
        atts.append(jnp.where(causal, att, 0.0).astype(BF16))

    outs = [(_rms_norm(o + _dot(att, v), ng) * r).astype(BF16) for o, att, v, r in zip(inter, atts, vs, rs)]
    return outs, new_sts


def _gla_kernel(*refs, chunk, nchunks, nt, heads, hk, hv, has_s0):
    if has_s0:
        q_ref, k_ref, g_ref, v_ref, r_ref, ng_ref, s0_ref, sall_ref, o_ref, sout_ref, st_scr = refs
    else:
        q_ref, k_ref, g_ref, v_ref, r_ref, ng_ref, sall_ref, o_ref, sout_ref, st_scr = refs
    del sall_ref
    t = pl.program_id(1)

    @pl.when(t == 0)
    def _():
        for h in range(heads):
            st_scr[h] = s0_ref[h].T if has_s0 else jnp.zeros((hv, hk), F32)

    row = lax.broadcasted_iota(jnp.int32, (chunk, chunk), 0)
    col = lax.broadcasted_iota(jnp.int32, (chunk, chunk), 1)
    causal = row >= col
    tril = causal.astype(F32)
    sub_lane = lax.broadcasted_iota(jnp.int32, (GLA_SUB, chunk), 1)
    ng = ng_ref[...]

    def body(c, carry):
        rows = pl.ds(pl.multiple_of(c * chunk, chunk), chunk)
        kcs = [slice(h * hk, (h + 1) * hk) for h in range(heads)]
        vcs = [slice(h * hv, (h + 1) * hv) for h in range(heads)]
        outs, new_sts = _gla_chunk([q_ref[rows, kc] for kc in kcs], [k_ref[rows, kc] for kc in kcs],
                                   g_ref[rows, :], [v_ref[rows, vc] for vc in vcs],
                                   [r_ref[rows, vc] for vc in vcs], [st_scr[h] for h in range(heads)],
                                   ng, tril, causal, sub_lane)
        for h in range(heads):
            o_ref[rows, vcs[h]] = outs[h]
            st_scr[h] = new_sts[h]
        return carry

    lax.fori_loop(0, nchunks, body, 0)

    @pl.when(t == nt - 1)
    def _():
        for h in range(heads):
            sout_ref[h] = st_scr[h].T


def _gla(q, k, g, v, r, ng, s_all, layer, s0_all=None, *, rows_per_batch, tile, chunk):
    m = q.shape[0]
    _, b, h, hk, hv = s_all.shape
    nt = rows_per_batch // tile
    assert m == b * rows_per_batch and tile % chunk == 0 and chunk % GLA_SUB == 0
    kspec = pl.BlockSpec((tile, h * hk), lambda bi, ti: (bi * nt + ti, 0))
    vspec = pl.BlockSpec((tile, h * hv), lambda bi, ti: (bi * nt + ti, 0))
    sspec = pl.BlockSpec((None, None, h, hk, hv), lambda bi, ti: (layer, bi, 0, 0, 0))
    in_specs = [kspec, kspec, kspec, vspec, vspec, pl.BlockSpec((1, hv), lambda bi, ti: (0, 0))]
    args = [q, k, g, v, r, ng]
    if s0_all is not None:
        in_specs.append(sspec)
        args.append(s0_all)
    in_specs.append(pl.BlockSpec(memory_space=pl.ANY))
    args.append(s_all)
    return pl.pallas_call(
        functools.partial(_gla_kernel, chunk=chunk, nchunks=tile // chunk, nt=nt, heads=h, hk=hk, hv=hv,
                          has_s0=s0_all is not None),
        grid=(b, nt),
        in_specs=in_specs,
        out_specs=[vspec, sspec],
        out_shape=[jax.ShapeDtypeStruct((m, h * hv), BF16), jax.ShapeDtypeStruct(s_all.shape, F32)],
        scratch_shapes=[pltpu.VMEM((h, hv, hk), F32)],
        input_output_aliases={len(args) - 1: 1},
        compiler_params=_cparams(("parallel", "arbitrary")),
        name="gla",
    )(*args)


def _lambda(lam_ref, lam_init):
    lf = lam_ref[...]
    a = jnp.sum(lf[0:1] * lf[1:2], axis=-1, keepdims=True)
    b = jnp.sum(lf[2:3] * lf[3:4], axis=-1, keepdims=True)
    return jnp.exp(a) - jnp.exp(b) + lam_init


def _attn_prompt_kernel(q_ref, kt_ref, v_ref, ng_ref, lam_ref, o_ref, qm_scr, vx_scr, m_scr, acc_scr,
                        *, tq, tk, td, tr, hd, lam_init):
    qi = pl.program_id(2)
    w = 2 * hd

    @pl.when(qi == 0)
    def _():
        vx_scr[:, 0:w] = v_ref[...]
        vx_scr[:, w:2 * w] = jnp.ones((v_ref.shape[0], w), BF16)

    q = q_ref[...]
    lane = lax.broadcasted_iota(jnp.int32, q.shape, 1)
    zero = jnp.zeros_like(q)
    qm_scr[0] = jnp.where(lane < hd, q, zero)
    qm_scr[1] = jnp.where(lane >= hd, q, zero)
    m_scr[...] = jnp.full(m_scr.shape, -jnp.inf, F32)
    acc_scr[...] = jnp.zeros(acc_scr.shape, F32)

    def absorb(cols, kw, c, r0, s, diag_row):
        rows = slice(r0, r0 + tr)
        if diag_row is not None and r0 < diag_row + kw:
            r_i = lax.broadcasted_iota(jnp.int32, s.shape, 0) + (r0 - diag_row)
            c_i = lax.broadcasted_iota(jnp.int32, s.shape, 1)
            s = jnp.where(r_i >= c_i, s, -jnp.inf)
        m_prev = m_scr[c, rows]
        m_new = jnp.maximum(m_prev, jnp.max(s, axis=-1, keepdims=True))
        alpha = jnp.exp2(m_prev - m_new)
        p = jnp.exp2(s - jnp.tile(m_new, (1, kw // LANES)))
        acc_scr[c, rows] = (jnp.tile(alpha, (1, 2)) * acc_scr[c, rows]
                            + _dot(p.astype(BF16), vx_scr[cols, :]))
        m_scr[c, rows] = m_new

    def step(k0, kw, diag_row):
        cols = pl.ds(pl.multiple_of(k0, kw), kw)
        kb = kt_ref[:, cols]
        chains = [(c, r0) for r0 in range(diag_row or 0, tq, tr) for c in range(2)]
        s_all = [_dot(qm_scr[c, r0:r0 + tr], kb) for c, r0 in chains]
        for (c, r0), s in zip(chains, s_all):
            absorb(cols, kw, c, r0, s, diag_row)

    def body(j, carry):
        step(j * tk, tk, None)
        return carry

    lax.fori_loop(0, qi * (tq // tk), body, 0)
    for dblk in range(tq // td):
        step(qi * tq + dblk * td, td, dblk * td)

    lam = _lambda(lam_ref, lam_init)
    a0 = acc_scr[0]
    a1 = acc_scr[1]
    o = a0[:, 0:w] / a0[:, w:2 * w] - lam * (a1[:, 0:w] / a1[:, w:2 * w])
    o_ref[...] = (_rms_norm(o, ng_ref[...]) * (1.0 - lam_init)).astype(BF16)


def _attn_prompt(q, kt, v, ng, lam_p, *, batch, heads, hd, seq, tq, tk, td, tr, lam_init):
    m, w = q.shape
    nq = seq // tq
    assert 2 * hd == LANES and tq % tk == 0 and tq % td == 0 and td % tr == 0
    qspec = pl.BlockSpec((tq, 2 * hd), lambda b, h, i: (b * nq + i, h))
    return pl.pallas_call(
        functools.partial(_attn_prompt_kernel, tq=tq, tk=tk, td=td, tr=tr, hd=hd, lam_init=lam_init),
        grid=(batch, heads, nq),
        in_specs=[qspec,
                  pl.BlockSpec((None, 2 * hd, seq), lambda b, h, i: (b, h, 0)),
                  pl.BlockSpec((seq, 2 * hd), lambda b, h, i: (b, h)),
                  pl.BlockSpec((1, 2 * hd), lambda b, h, i: (0, 0)),
                  pl.BlockSpec((4, hd), lambda b, h, i: (0, 0))],
        out_specs=qspec,
        out_shape=jax.ShapeDtypeStruct((m, w), BF16),
        scratch_shapes=[pltpu.VMEM((2, tq, 2 * hd), BF16), pltpu.VMEM((seq, 4 * hd), BF16),
                        pltpu.VMEM((2, tq, 2 * hd), F32), pltpu.VMEM((2, tq, 4 * hd), F32)],
        compiler_params=_cparams(("parallel", "parallel", "arbitrary")),
        name="attn_prompt",
    )(q, kt, v, ng, lam_p)


def _attn_decode_kernel(pt_ref, qbd_ref, kn_ref, vn_ref, ng_ref, lam_ref, *refs, pg, dec_seq, heads, hd, lam_init):
    del pt_ref
    k_refs = refs[:pg]
    v_refs = refs[pg:2 * pg]
    o_ref, m_scr, l_scr, acc_scr = refs[2 * pg:]
    j = pl.program_id(1)
    nj = pl.num_programs(1)
    qbd = qbd_ref[...]
    grp = 2 * dec_seq
    page = kn_ref.shape[-1]

    def pv(p, v_ref):
        outs = []
        for h in range(heads):
            v_h = v_ref[pl.ds(h, page, stride=heads), :].astype(BF16)
            outs.append(_dot(p[h * grp:(h + 1) * grp], v_h))
        return jnp.concatenate(outs, axis=0)

    @pl.when(j == 0)
    def _():
        s = _dot(qbd, kn_ref[...])
        r_i = lax.broadcasted_iota(jnp.int32, s.shape, 0) % dec_seq
        c_i = lax.broadcasted_iota(jnp.int32, s.shape, 1)
        s = jnp.where(c_i <= r_i, s, -jnp.inf)
        m0 = jnp.max(s, axis=-1, keepdims=True)
        p = jnp.exp2(s - m0)
        m_scr[...] = jnp.broadcast_to(m0, m_scr.shape)
        l_scr[...] = jnp.broadcast_to(jnp.sum(p, axis=-1, keepdims=True), l_scr.shape)
        acc_scr[...] = pv(p.astype(BF16), vn_ref)

    s = jnp.concatenate([_dot(qbd, kr[...].astype(BF16)) for kr in k_refs], axis=-1)
    m_prev = m_scr[...]
    m_new = jnp.maximum(m_prev, jnp.max(s, axis=-1, keepdims=True))
    alpha = jnp.exp2(m_prev - m_new)
    p = jnp.exp2(s - jnp.tile(m_new, (1, s.shape[1] // LANES)))
    l_scr[...] = alpha * l_scr[...] + jnp.sum(p, axis=-1, keepdims=True)
    pb = p.astype(BF16)
    upd = pv(pb[:, 0:page], v_refs[0])
    for i in range(1, pg):
        upd = upd + pv(pb[:, i * page:(i + 1) * page], v_refs[i])
    acc_scr[...] = alpha * acc_scr[...] + upd
    m_scr[...] = m_new

    @pl.when(j == nj - 1)
    def _():
        lam = _lambda(lam_ref, lam_init)
        ng = ng_ref[...]
        n = acc_scr[...] / l_scr[...]
        for h in range(heads):
            blk = n[h * grp:(h + 1) * grp]
            o = blk[0:dec_seq] - lam * blk[dec_seq:grp]
            o_ref[:, h * 2 * hd:(h + 1) * 2 * hd] = _rms_norm(o, ng) * (1.0 - lam_init)


def _attn_decode(page_table, qbd, kt_new, v_new, ng, lam_p, cache_kt, cache_v2, layer, *, pg, dec_seq, heads, hd,
                 lam_init):
    db, r, w = qbd.shape
    n_pages = page_table.shape[1]
    page = cache_kt.shape[-1]
    assert n_pages % pg == 0 and page == LANES and 2 * hd == LANES

    def kspec(i):
        return pl.BlockSpec((None, None, w, page), lambda b, j, pt, i=i: (layer, pt[b, j * pg + i], 0, 0))

    def vspec(i):
        return pl.BlockSpec((None, None, page * heads, 2 * hd),
                            lambda b, j, pt, i=i: (layer, pt[b, j * pg + i], 0, 0))

    in_specs = [pl.BlockSpec((None, r, w), lambda b, j, pt: (b, 0, 0)),
                pl.BlockSpec((None, w, page), lambda b, j, pt: (b, 0, 0)),
                pl.BlockSpec((None, page * heads, 2 * hd), lambda b, j, pt: (b, 0, 0)),
                pl.BlockSpec((1, 2 * hd), lambda b, j, pt: (0, 0)),
                pl.BlockSpec((4, hd), lambda b, j, pt: (0, 0))]
    in_specs += [kspec(i) for i in range(pg)] + [vspec(i) for i in range(pg)]
    return pl.pallas_call(
        functools.partial(_attn_decode_kernel, pg=pg, dec_seq=dec_seq, heads=heads, hd=hd, lam_init=lam_init),
        grid_spec=pltpu.PrefetchScalarGridSpec(
            num_scalar_prefetch=1,
            grid=(db, n_pages // pg),
            in_specs=in_specs,
            out_specs=pl.BlockSpec((None, dec_seq, w), lambda b, j, pt: (b, 0, 0)),
            scratch_shapes=[pltpu.VMEM((r, 2 * hd), F32)] * 3),
        out_shape=jax.ShapeDtypeStruct((db, dec_seq, w), F32),
        compiler_params=_cparams(("parallel", "arbitrary")),
        name="attn_decode",
    )(page_table, qbd, kt_new, v_new, ng, lam_p, *([cache_kt] * pg), *([cache_v2] * pg))


def _mix_kernel(x_ref, sh_ref, sc_ref, gt_ref, oa_ref, ob_ref, wga_ref, wgb_ref, wa_ref, wb_ref, wo_ref,
                g_ref, b_ref, o_ref, *, alpha):
    x = x_ref[...]
    u = (x * (1.0 + sc_ref[...]) + sh_ref[...]).astype(BF16)
    ya = _sigmoid(_dot(u, wga_ref[...])) * _dot(oa_ref[...], wa_ref[...])
    yb = _sigmoid(_dot(u, wgb_ref[...])) * _dot(ob_ref[...], wb_ref[...])
    z = _dot((ya + yb).astype(BF16), wo_ref[...])
    o_ref[...] = _layer_norm(alpha * x + gt_ref[...] * z, g_ref[...], b_ref[...])


def _mix(x, mod, layer, oa, ob, wgate, wa, wb, wo, ln_g, ln_b, *, alpha, tm, rows_per_batch):
    m, d = x.shape
    row = pl.BlockSpec((tm, d), lambda i: (i, 0))
    wsq = _resident((None, d, d), lambda i: (layer, 0, 0))
    in_specs = [row] + _mod_specs(mod, layer, 1, tm, rows_per_batch, (0, 1, 2)) + [
        row, row,
        _resident((None, d, d), lambda i: (layer, 0, 0)),
        _resident((None, d, d), lambda i: (layer, 0, 1)),
        wsq, wsq, wsq,
        pl.BlockSpec((None, None, 1, d), lambda i: (layer, 1, 0, 0)),
        pl.BlockSpec((None, None, 1, d), lambda i: (layer, 1, 0, 0)),
    ]
    return pl.pallas_call(
        functools.partial(_mix_kernel, alpha=alpha),
        grid=(m // tm,),
        in_specs=in_specs,
        out_specs=row,
        out_shape=jax.ShapeDtypeStruct((m, d), F32),
        compiler_params=_cparams(("parallel",)),
        name="mix",
    )(x, mod, mod, mod, oa, ob, wgate, wgate, wa, wb, wo, ln_g, ln_b)


def _rope_angles(pos, hd):
    half = hd // 2
    inv = ROPE_THETA ** (-jnp.arange(half, dtype=F32) * (2.0 / hd))
    ang = pos.astype(F32)[:, None] * inv[None, :]
    return jnp.cos(ang), jnp.sin(ang)


def _rope_lane_tables(cos, sin, hd):
    reps = LANES // hd
    return (jnp.tile(jnp.concatenate([cos, cos], -1), (1, reps)),
            jnp.tile(jnp.concatenate([-sin, sin], -1), (1, reps)))


def _tile(n, pref):
    t = min(n, pref)
    while n % t:
        t //= 2
    return t


MXU_WIDTH = 256


def _mxu_chunk(n):
    return MXU_WIDTH if n % MXU_WIDTH == 0 else LANES


def kernel(x_prompt, x_sample, cache_k, cache_v, state_gla, page_table, c_prompt, c_sample, w_ada, b_ada,
           ln_g, ln_b, ffn_w_in, ffn_w_out, w_in, gla_w_f2, gla_b_f2, gla_norm_g, diff_lam, diff_norm_g,
           w_branch_a, w_branch_b, w_out):
    batch, seq, d = x_prompt.shape
    db, dec_seq, _ = x_sample.shape
    depth = w_ada.shape[0]
    gla_heads, hk, hv = state_gla.shape[2:]
    n_pool, page, heads, _, hd = cache_k.shape[1:]
    dk, dv = gla_heads * hk, gla_heads * hv
    wq = 2 * heads * hd
    rank = gla_w_f2.shape[1]
    f = ffn_w_out.shape[2]
    past = page_table.shape[1] * page
    alpha = (2.0 * depth) ** 0.25
    mp, ms = batch * seq, db * dec_seq

    o_f = 2 * dk + 2 * dv
    o_dq = o_f + rank
    o_ga = o_dq + 3 * wq
    wg = w_in[:, :, 0:o_f].astype(BF16)
    rank_pad = -(-rank // LANES) * LANES
    wf1 = jnp.pad(w_in[:, :, o_f:o_dq], ((0, 0), (0, 0), (0, rank_pad - rank))).astype(BF16)
    wf2 = jnp.pad(gla_w_f2, ((0, 0), (0, rank_pad - rank), (0, 0))).astype(BF16)
    bf2 = gla_b_f2.reshape(depth, 1, dk)
    wd = w_in[:, :, o_dq:o_ga].astype(BF16)
    wkt = jnp.swapaxes(w_in[:, :, o_dq + wq:o_dq + 2 * wq], 1, 2).astype(BF16)
    wgate = w_in[:, :, o_ga:o_ga + 2 * d].astype(BF16)
    ffn_in = ffn_w_in.astype(BF16)
    ffn_out = ffn_w_out.astype(BF16)
    wa = w_branch_a.astype(BF16)
    wb = w_branch_b.astype(BF16)
    wo = w_out.astype(BF16)
    ln_g4 = ln_g.reshape(depth, 3, 1, d)
    ln_b4 = ln_b.reshape(depth, 3, 1, d)
    gla_ng = gla_norm_g.reshape(depth, 1, hv)
    diff_ng = diff_norm_g.reshape(depth, 1, 2 * hd)

    n_c = batch + db
    n_c_pad = -(-n_c // 8) * 8
    c_all = jnp.pad(jnp.concatenate([c_prompt, c_sample], 0), ((0, n_c_pad - n_c), (0, 0)))
    mod = _ada_mod(c_all, w_ada, b_ada)
    mod_p = mod[:, :batch].reshape(depth, batch, 1, N_MOD * d)
    mod_s = jnp.repeat(mod[:, batch:n_c], dec_seq, axis=1)

    cos_p, sin_p = _rope_angles(jnp.arange(seq, dtype=jnp.int32), hd)
    cos_pl, sin_pl = _rope_lane_tables(cos_p, sin_p, hd)
    cos_pt, sin_pt = cos_p.T, sin_p.T
    cos_s, sin_s = _rope_lane_tables(*_rope_angles(past + jnp.arange(dec_seq, dtype=jnp.int32), hd), hd)
    cos_s = jnp.tile(cos_s, (db, 1))
    sin_s = jnp.tile(sin_s, (db, 1))

    tm_p = _tile(seq, 512)
    fc = _mxu_chunk(f)
    tq = _tile(seq, 1024)
    tk = tq
    td = _tile(tq, 512)
    tr = td
    gla_tile = _tile(seq, 512)
    gla_chunk = _tile(gla_tile, 64)
    dec_pad = -(-dec_seq // BF16_ROWS) * BF16_ROWS
    pg = _tile(page_table.shape[1], 16)
    cache_kt = cache_k.transpose(0, 1, 3, 4, 5, 2).reshape(depth, n_pool, wq, page)
    cache_v2 = cache_v.reshape(depth, n_pool, page * heads, 2 * hd)
    eye_hc = jnp.eye(2 * heads, dtype=BF16)

    hp = x_prompt.reshape(mp, d)
    hs = x_sample.reshape(ms, d)
    k_all = jnp.zeros((depth, batch, wq, seq), F32)
    v_all = jnp.zeros((depth, mp * heads, 2 * hd), F32)
    sp_all = jnp.zeros((depth, batch, gla_heads, hk, hv), F32)
    sd_all = jnp.zeros((depth, db, gla_heads, hk, hv), F32)
    kd, vd = [], []

    def pad_rows(t):
        t3 = t.reshape(db, dec_seq, t.shape[-1])
        return jnp.pad(t3, ((0, 0), (0, dec_pad - dec_seq), (0, 0))).reshape(db * dec_pad, t.shape[-1])

    for l in range(depth):
        lam_init = 0.8 - 0.6 * math.exp(-0.3 * l)

        hp = _ffn(hp, mod_p, l, 0, 0, ffn_in, ffn_out, ln_g4, ln_b4, alpha=alpha, tm=tm_p, fc=fc,
                  rows_per_batch=seq)
        gq, gk, gv, gr, gg = _proj_gla(hp, mod_p, l, wg, wf1, wf2, bf2, dk=dk, dv=dv, hk=hk, tm=tm_p,
                                       rows_per_batch=seq)
        k_all, v_all, dq, ktb, vb = _proj_diff_cols(hp, mod_p, l, wd, wkt, cos_pl, sin_pl, cos_pt, sin_pt,
                                                    k_all, v_all, hd=hd, heads=heads, tm=tm_p, seq=seq)
        oa, sp_all = _gla(gq, gk, gg, gv, gr, gla_ng[l], sp_all, l, rows_per_batch=seq, tile=gla_tile,
                          chunk=gla_chunk)
        ob = _attn_prompt(dq, ktb, vb, diff_ng[l], diff_lam[l], batch=batch, heads=heads, hd=hd, seq=seq,
                          tq=tq, tk=tk, td=td, tr=tr, lam_init=lam_init)
        hp = _mix(hp, mod_p, l, oa, ob, wgate, wa, wb, wo, ln_g4, ln_b4, alpha=alpha, tm=tm_p,
                  rows_per_batch=seq)
        hp = _ffn(hp, mod_p, l, 2, 1, ffn_in, ffn_out, ln_g4, ln_b4, alpha=alpha, tm=tm_p, fc=fc,
                  rows_per_batch=seq)

        hs = _ffn(hs, mod_s, l, 0, 0, ffn_in, ffn_out, ln_g4, ln_b4, alpha=alpha, tm=ms, fc=fc,
                  rows_per_batch=None)
        gq, gk, gv, gr, gg = _proj_gla(hs, mod_s, l, wg, wf1, wf2, bf2, dk=dk, dv=dv, hk=hk, tm=ms,
                                       rows_per_batch=None)
        dq, kf, kb, vf = _proj_diff_rows(hs, mod_s, l, wd, cos_s, sin_s, hd=hd, tm=ms)
        oa, sd_all = _gla(pad_rows(gq), pad_rows(gk), pad_rows(gg), pad_rows(gv), pad_rows(gr), gla_ng[l],
                          sd_all, l, state_gla, rows_per_batch=dec_pad, tile=dec_pad, chunk=dec_pad)
        oa = oa.reshape(db, dec_pad, dv)[:, :dec_seq].reshape(ms, dv)
        q5 = dq.reshape(db, dec_seq, 2 * heads, hd)
        qbd = (q5.transpose(0, 2, 1, 3)[:, :, :, None, :] * eye_hc[None, :, None, :, None]).reshape(
            db, 2 * heads * dec_seq, wq)
        kt_new = jnp.pad(kb.reshape(db, dec_seq, wq).transpose(0, 2, 1), ((0, 0), (0, 0), (0, page - dec_seq)))
        v_new = jnp.pad(vf.reshape(db, dec_seq, wq), ((0, 0), (0, page - dec_seq), (0, 0))).reshape(
            db, page * heads, 2 * hd)
        ob = _attn_decode(page_table, qbd, kt_new, v_new, diff_ng[l], diff_lam[l], cache_kt, cache_v2, l, pg=pg,
                          dec_seq=dec_seq, heads=heads, hd=hd, lam_init=lam_init)
        ob = ob.reshape(ms, wq).astype(BF16)
        hs = _mix(hs, mod_s, l, oa, ob, wgate, wa, wb, wo, ln_g4, ln_b4, alpha=alpha, tm=ms,
                  rows_per_batch=None)
        hs = _ffn(hs, mod_s, l, 2, 1, ffn_in, ffn_out, ln_g4, ln_b4, alpha=alpha, tm=ms, fc=fc,
                  rows_per_batch=None)
        kd.append(kf)
        vd.append(vf)

    k_prompt = k_all.reshape(depth, batch, heads, 2, hd, seq).transpose(0, 1, 5, 2, 3, 4)
    return (hp.reshape(batch, seq, d),
            hs.reshape(db, dec_seq, d),
            k_prompt,
            v_all.reshape(depth, batch, seq, heads, 2 * hd),
            sp_all,
            jnp.stack(kd).reshape(depth, db, dec_seq, heads, 2, hd),
            jnp.stack(vd).reshape(depth, db, dec_seq, heads, 2 * hd),
            sd_all)
```

```python
import functools
import math

import jax
import jax.numpy as jnp
from jax import lax
from jax.experimental import pallas as pl
from jax.experimental.pallas import tpu as pltpu

F32 = jnp.float32
BF16 = jnp.bfloat16

LN_EPS = 1e-5
GLA_TAU = 16.0
ROPE_THETA = 10000.0
N_MOD = 9
LANES = 128
GLA_SUB = 8
BF16_ROWS = 16
VMEM_LIMIT_BYTES = 56 * 1024 * 1024
LOG2E = math.log2(math.e)


def _cparams(sem):
    return pltpu.CompilerParams(dimension_semantics=sem, vmem_limit_bytes=VMEM_LIMIT_BYTES)


def _dot(a, b):
    return jnp.dot(a, b, preferred_element_type=F32)


def _dot_nt(a, b):
    return lax.dot_general(a, b, (((1,), (1,)), ((), ())), preferred_element_type=F32)


def _dot_tn(a, b):
    return lax.dot_general(a, b, (((0,), (0,)), ((), ())), preferred_element_type=F32)


def _sigmoid(x):
    return 1.0 / (1.0 + jnp.exp(-x))


def _silu(x):
    return x * _sigmoid(x)


def _layer_norm(y, g, b):
    mu = jnp.mean(y, axis=-1, keepdims=True)
    yc = y - mu
    var = jnp.mean(yc * yc, axis=-1, keepdims=True)
    return yc * lax.rsqrt(var + LN_EPS) * g + b


def _rms_norm(o, g):
    return o * lax.rsqrt(jnp.mean(o * o, axis=-1, keepdims=True) + LN_EPS) * g


def _modulated(x_ref, sh_ref, sc_ref):
    return (x_ref[...] * (1.0 + sc_ref[...]) + sh_ref[...]).astype(BF16)


def _mod_specs(mod, layer, sub, tm, rows_per_batch, which):
    d = mod.shape[-1] // N_MOD
    specs = []
    for w in which:
        col = 3 * sub + w
        if rows_per_batch is not None:
            tiles = rows_per_batch // tm
            specs.append(pl.BlockSpec((None, None, 1, d),
                                      lambda i, *_, col=col, tiles=tiles: (layer, i // tiles, 0, col)))
        else:
            specs.append(pl.BlockSpec((None, tm, d), lambda i, *_, col=col: (layer, i, col)))
    return specs


def _ada_kernel(c_ref, w_ref, b_ref, o_ref):
    a = _silu(c_ref[...]).astype(BF16)
    o_ref[...] = _dot(a, w_ref[...].astype(BF16)) + b_ref[...]


def _ada_mod(c_all, w_ada, b_ada):
    depth, d, n = w_ada.shape
    r = c_all.shape[0]
    tn = d
    return pl.pallas_call(
        _ada_kernel,
        grid=(depth, n // tn),
        in_specs=[pl.BlockSpec((r, d), lambda l, j: (0, 0)),
                  pl.BlockSpec((None, d, tn), lambda l, j: (l, 0, j)),
                  pl.BlockSpec((None, 1, tn), lambda l, j: (l, 0, j))],
        out_specs=pl.BlockSpec((None, r, tn), lambda l, j: (l, 0, j)),
        out_shape=jax.ShapeDtypeStruct((depth, r, n), F32),
        compiler_params=_cparams(("parallel", "parallel")),
        name="ada_mod",
    )(c_all, w_ada, b_ada.reshape(depth, 1, n))


def _ffn_kernel(x_ref, sh_ref, sc_ref, gt_ref, wi_ref, wo_ref, g_ref, b_ref, o_ref, *, alpha, f, fc):
    x = x_ref[...]
    u = (x * (1.0 + sc_ref[...]) + sh_ref[...]).astype(BF16)

    def up(k):
        return (_dot(u, wi_ref[:, k * fc:(k + 1) * fc]), _dot(u, wi_ref[:, f + k * fc:f + (k + 1) * fc]))

    nk = f // fc
    cur = up(0)
    acc = None
    for k in range(nk):
        nxt = up(k + 1) if k + 1 < nk else None
        part = _dot((_silu(cur[0]) * cur[1]).astype(BF16), wo_ref[k * fc:(k + 1) * fc, :])
        acc = part if acc is None else acc + part
        cur = nxt
    y = alpha * x + gt_ref[...] * (0.5 * acc)
    o_ref[...] = _layer_norm(y, g_ref[...], b_ref[...])


def _resident(block_shape, index_map):
    return pl.BlockSpec(block_shape, index_map, pipeline_mode=pl.Buffered(1))


def _ffn(x, mod, layer, sub, which_ffn, w_in, w_out, ln_g, ln_b, *, alpha, tm, fc, rows_per_batch):
    m, d = x.shape
    f = w_out.shape[2]
    assert f % fc == 0
    in_specs = [pl.BlockSpec((tm, d), lambda i: (i, 0))]
    in_specs += _mod_specs(mod, layer, sub, tm, rows_per_batch, (0, 1, 2))
    in_specs += [
        _resident((None, None, d, 2 * f), lambda i: (layer, which_ffn, 0, 0)),
        _resident((None, None, f, d), lambda i: (layer, which_ffn, 0, 0)),
        pl.BlockSpec((None, None, 1, d), lambda i: (layer, sub, 0, 0)),
        pl.BlockSpec((None, None, 1, d), lambda i: (layer, sub, 0, 0)),
    ]
    return pl.pallas_call(
        functools.partial(_ffn_kernel, alpha=alpha, f=f, fc=fc),
        grid=(m // tm,),
        in_specs=in_specs,
        out_specs=pl.BlockSpec((tm, d), lambda i: (i, 0)),
        out_shape=jax.ShapeDtypeStruct((m, d), F32),
        compiler_params=_cparams(("parallel",)),
        name="ffn",
    )(x, mod, mod, mod, w_in, w_out, ln_g, ln_b)


def _proj_gla_kernel(x_ref, sh_ref, sc_ref, wg_ref, wf1_ref, wf2_ref, bf2_ref,
                     q_ref, k_ref, v_ref, r_ref, g_ref, *, dk, dv, qscale):
    u = _modulated(x_ref, sh_ref, sc_ref)
    q_ref[...] = _dot(u, wg_ref[:, 0:dk]) * qscale
    k_ref[...] = _dot(u, wg_ref[:, dk:2 * dk])
    v_ref[...] = _dot(u, wg_ref[:, 2 * dk:2 * dk + dv]).astype(BF16)
    r_ref[...] = _silu(_dot(u, wg_ref[:, 2 * dk + dv:2 * dk + 2 * dv]))
    gf = _dot(u, wf1_ref[...])
    z = _dot(gf.astype(BF16), wf2_ref[...]) + bf2_ref[...]
    g_ref[...] = (jnp.minimum(z, 0.0) - jnp.log1p(jnp.exp(-jnp.abs(z)))) * (1.0 / GLA_TAU)


def _proj_gla(x, mod, layer, wg, wf1, wf2, bf2, *, dk, dv, hk, tm, rows_per_batch):
    m, d = x.shape
    rp = wf1.shape[-1]
    in_specs = [pl.BlockSpec((tm, d), lambda i: (i, 0))]
    in_specs += _mod_specs(mod, layer, 1, tm, rows_per_batch, (0, 1))
    in_specs += [
        pl.BlockSpec((None, d, 2 * dk + 2 * dv), lambda i: (layer, 0, 0)),
        pl.BlockSpec((None, d, rp), lambda i: (layer, 0, 0)),
        pl.BlockSpec((None, rp, dk), lambda i: (layer, 0, 0)),
        pl.BlockSpec((None, 1, dk), lambda i: (layer, 0, 0)),
    ]
    out_shapes = [jax.ShapeDtypeStruct((m, dk), F32), jax.ShapeDtypeStruct((m, dk), F32),
                  jax.ShapeDtypeStruct((m, dv), BF16), jax.ShapeDtypeStruct((m, dv), F32),
                  jax.ShapeDtypeStruct((m, dk), F32)]
    return pl.pallas_call(
        functools.partial(_proj_gla_kernel, dk=dk, dv=dv, qscale=hk ** -0.5),
        grid=(m // tm,),
        in_specs=in_specs,
        out_specs=[pl.BlockSpec((tm, s.shape[1]), lambda i: (i, 0)) for s in out_shapes],
        out_shape=out_shapes,
        compiler_params=_cparams(("parallel",)),
        name="proj_gla",
    )(x, mod, mod, wg, wf1, wf2, bf2)


def _rope_lanes(t, cos, sin, first_half, hd):
    rot = jnp.where(first_half, pltpu.roll(t, LANES - hd // 2, 1), pltpu.roll(t, hd // 2, 1))
    return t * cos + rot * sin


def _proj_q(u, wd_ref, cos, sin, first_half, q_ref, w, hd, qscale):
    dq = _dot(u, wd_ref[:, 0:w])
    for c in range(w // LANES):
        sl = slice(c * LANES, (c + 1) * LANES)
        q_ref[:, sl] = (_rope_lanes(dq[:, sl], cos, sin, first_half, hd) * qscale).astype(BF16)


def _proj_diff_rows_kernel(x_ref, sh_ref, sc_ref, wd_ref, cos_ref, sin_ref,
                           q_ref, kf_ref, kb_ref, vf_ref, *, w, hd, qscale):
    u = _modulated(x_ref, sh_ref, sc_ref)
    cos = cos_ref[...]
    sin = sin_ref[...]
    first_half = (lax.broadcasted_iota(jnp.int32, cos.shape, 1) % hd) < (hd // 2)
    _proj_q(u, wd_ref, cos, sin, first_half, q_ref, w, hd, qscale)
    dk = _dot(u, wd_ref[:, w:2 * w])
    for c in range(w // LANES):
        sl = slice(c * LANES, (c + 1) * LANES)
        kr = _rope_lanes(dk[:, sl], cos, sin, first_half, hd)
        kf_ref[:, sl] = kr
        kb_ref[:, sl] = kr.astype(BF16)
    vf_ref[...] = _dot(u, wd_ref[:, 2 * w:3 * w])


def _proj_diff_cols_kernel(x_ref, sh_ref, sc_ref, wd_ref, wkt_ref, cos_ref, sin_ref, cost_ref, sint_ref, *refs,
                           w, hd, heads, qscale):
    kt_ref, vf_ref, q_ref, ktb_ref, vb_ref = refs[-5:]
    u = _modulated(x_ref, sh_ref, sc_ref)
    cos = cos_ref[...]
    sin = sin_ref[...]
    first_half = (lax.broadcasted_iota(jnp.int32, cos.shape, 1) % hd) < (hd // 2)
    _proj_q(u, wd_ref, cos, sin, first_half, q_ref, w, hd, qscale)
    dkt = _dot_nt(wkt_ref[...], u)
    cost = cost_ref[...]
    sint = sint_ref[...]
    half = hd // 2
    for g in range(w // hd):
        x1 = dkt[g * hd:g * hd + half]
        x2 = dkt[g * hd + half:(g + 1) * hd]
        r1 = x1 * cost - x2 * sint
        r2 = x2 * cost + x1 * sint
        kt_ref[g * hd:g * hd + half, :] = r1
        kt_ref[g * hd + half:(g + 1) * hd, :] = r2
        ktb_ref[g * hd:g * hd + half, :] = r1.astype(BF16)
        ktb_ref[g * hd + half:(g + 1) * hd, :] = r2.astype(BF16)
    dv = _dot(u, wd_ref[:, 2 * w:3 * w])
    tm = dv.shape[0]
    for h in range(heads):
        vf_ref[pl.ds(h, tm, stride=heads), :] = dv[:, h * 2 * hd:(h + 1) * 2 * hd]
    vb_ref[...] = dv.astype(BF16)


def _proj_diff_rows(x, mod, layer, wd, cos_t, sin_t, *, hd, tm):
    m, d = x.shape
    w = wd.shape[-1] // 3
    in_specs = [pl.BlockSpec((tm, d), lambda i: (i, 0))]
    in_specs += _mod_specs(mod, layer, 1, tm, None, (0, 1))
    in_specs += [
        pl.BlockSpec((None, d, 3 * w), lambda i: (layer, 0, 0)),
        pl.BlockSpec((tm, LANES), lambda i: (i, 0)),
        pl.BlockSpec((tm, LANES), lambda i: (i, 0)),
    ]
    out_shapes = [jax.ShapeDtypeStruct((m, w), BF16), jax.ShapeDtypeStruct((m, w), F32),
                  jax.ShapeDtypeStruct((m, w), BF16), jax.ShapeDtypeStruct((m, w), F32)]
    return pl.pallas_call(
        functools.partial(_proj_diff_rows_kernel, w=w, hd=hd, qscale=hd ** -0.5 * LOG2E),
        grid=(m // tm,),
        in_specs=in_specs,
        out_specs=[pl.BlockSpec((tm, w), lambda i: (i, 0)) for _ in out_shapes],
        out_shape=out_shapes,
        compiler_params=_cparams(("parallel",)),
        name="proj_diff_rows",
    )(x, mod, mod, wd, cos_t, sin_t)


def _proj_diff_cols(x, mod, layer, wd, wkt, cos_t, sin_t, cos_tt, sin_tt, k_all, v_all, *, hd, heads, tm, seq):
    fresh = isinstance(k_all, jax.ShapeDtypeStruct)
    m, d = x.shape
    w = wkt.shape[1]
    npos = seq // tm
    batch = m // seq
    in_specs = [pl.BlockSpec((tm, d), lambda i: (i, 0))]
    in_specs += _mod_specs(mod, layer, 1, tm, seq, (0, 1))
    in_specs += [
        pl.BlockSpec((None, d, 3 * w), lambda i: (layer, 0, 0)),
        pl.BlockSpec((None, w, d), lambda i: (layer, 0, 0)),
        pl.BlockSpec((tm, LANES), lambda i: (i % npos, 0)),
        pl.BlockSpec((tm, LANES), lambda i: (i % npos, 0)),
        pl.BlockSpec((hd // 2, tm), lambda i: (0, i % npos)),
        pl.BlockSpec((hd // 2, tm), lambda i: (0, i % npos)),
    ]
    args = [x, mod, mod, wd, wkt, cos_t, sin_t, cos_tt, sin_tt]
    aliases = {}
    if not fresh:
        aliases = {len(args): 0, len(args) + 1: 1}
        in_specs += [pl.BlockSpec(memory_space=pl.ANY), pl.BlockSpec(memory_space=pl.ANY)]
        args += [k_all, v_all]
    out_shapes = [jax.ShapeDtypeStruct(k_all.shape, F32), jax.ShapeDtypeStruct(v_all.shape, F32),
                  jax.ShapeDtypeStruct((m, w), BF16),
                  jax.ShapeDtypeStruct((batch, w, seq), BF16), jax.ShapeDtypeStruct((m, w), BF16)]
    out_specs = [
        pl.BlockSpec((None, None, w, tm), lambda i: (layer, i // npos, 0, i % npos)),
        pl.BlockSpec((None, tm * heads, 2 * hd), lambda i: (layer, i, 0)),
        pl.BlockSpec((tm, w), lambda i: (i, 0)),
        pl.BlockSpec((None, w, tm), lambda i: (i // npos, 0, i % npos)),
        pl.BlockSpec((tm, w), lambda i: (i, 0)),
    ]
    return pl.pallas_call(
        functools.partial(_proj_diff_cols_kernel, w=w, hd=hd, heads=heads, qscale=hd ** -0.5 * LOG2E),
        grid=(m // tm,),
        in_specs=in_specs,
        out_specs=out_specs,
        out_shape=out_shapes,
        input_output_aliases=aliases,
        compiler_params=_cparams(("parallel",)),
        name="proj_diff_cols",
    )(*args)


def _gla_chunk(qs, ks, g_all, vs, rs, sts, ng, tril, causal, sub_lane):
    heads = len(qs)
    c_rows, hk = qs[0].shape
    nsub = c_rows // GLA_SUB
    cum_all = jnp.dot(tril, g_all * LOG2E, precision=lax.Precision.HIGHEST, preferred_element_type=F32)
    cums = [cum_all[:, h * hk:(h + 1) * hk] for h in range(heads)]
    lasts = [cum[c_rows - 1:c_rows] for cum in cums]

    inter = [_dot_nt((q * jnp.exp2(cum)).astype(BF16), st.astype(BF16)) for q, cum, st in zip(qs, cums, sts)]
    offs = []
    for q, k, cum in zip(qs, ks, cums):
        per_block = [None]
        for i in range(1, nsub):
            lo = i * GLA_SUB
            base = cum[lo - 1:lo]
            qt = (q[lo:lo + GLA_SUB] * jnp.exp2(cum[lo:lo + GLA_SUB] - base)).astype(BF16)
            kt = jnp.concatenate([(k[0:lo] * jnp.exp2(base - cum[0:lo])).astype(F32),
                                  jnp.zeros((c_rows - lo, hk), F32)], axis=0).astype(BF16)
            per_block.append(_dot_nt(qt, kt))
        offs.append(per_block)
    new_sts = [st * jnp.exp2(last) + _dot_tn(v, (k * jnp.exp2(last - cum)).astype(BF16))
               for k, v, cum, last, st in zip(ks, vs, cums, lasts, sts)]

    atts = []
    for q, k, cum, off in zip(qs, ks, cums, offs):
        blocks = []
        for i in range(nsub):
            lo = i * GLA_SUB
            qi = q[lo:lo + GLA_SUB]
            ci = cum[lo:lo + GLA_SUB]
            acc = jnp.zeros((GLA_SUB, c_rows), F32)
            for s in range(GLA_SUB):
                dec = jnp.exp2(jnp.minimum(ci - ci[s:s + 1], 0.0))
                a_col = jnp.sum(qi * k[lo + s:lo + s + 1] * dec, axis=-1, keepdims=True)
                acc = jnp.where(sub_lane == lo + s, a_col, acc)
            if i > 0:
                acc = jnp.where(sub_lane < lo, off[i], acc)
            blocks.append(acc)
        att = blocks[0] if nsub == 1 else jnp.concatenate(blocks, axis=0)
        atts.append(jnp.where(causal, att, 0.0).astype(BF16))

    outs = [(_rms_norm(o + _dot(att, v), ng) * r).astype(BF16) for o, att, v, r in zip(inter, atts, vs, rs)]
    return outs, new_sts


def _gla_kernel(*refs, chunk, nchunks, nt, heads, hk, hv, has_s0):
    q_ref, k_ref, g_ref, v_ref, r_ref, ng_ref = refs[:6]
    s0_ref = refs[6] if has_s0 else None
    o_ref, sout_ref, st_scr = refs[-3:]
    t = pl.program_id(1)

    @pl.when(t == 0)
    def _():
        for h in range(heads):
            st_scr[h] = s0_ref[h].T if has_s0 else jnp.zeros((hv, hk), F32)

    row = lax.broadcasted_iota(jnp.int32, (chunk, chunk), 0)
    col = lax.broadcasted_iota(jnp.int32, (chunk, chunk), 1)
    causal = row >= col
    tril = causal.astype(F32)
    sub_lane = lax.broadcasted_iota(jnp.int32, (GLA_SUB, chunk), 1)
    ng = ng_ref[...]

    def body(c, carry):
        rows = pl.ds(pl.multiple_of(c * chunk, chunk), chunk)
        kcs = [slice(h * hk, (h + 1) * hk) for h in range(heads)]
        vcs = [slice(h * hv, (h + 1) * hv) for h in range(heads)]
        outs, new_sts = _gla_chunk([q_ref[rows, kc] for kc in kcs], [k_ref[rows, kc] for kc in kcs],
                                   g_ref[rows, :], [v_ref[rows, vc] for vc in vcs],
                                   [r_ref[rows, vc] for vc in vcs], [st_scr[h] for h in range(heads)],
                                   ng, tril, causal, sub_lane)
        for h in range(heads):
            o_ref[rows, vcs[h]] = outs[h]
            st_scr[h] = new_sts[h]
        return carry

    lax.fori_loop(0, nchunks, body, 0)

    @pl.when(t == nt - 1)
    def _():
        for h in range(heads):
            sout_ref[h] = st_scr[h].T


def _gla(q, k, g, v, r, ng, s_all, layer, s0_all=None, *, rows_per_batch, tile, chunk):
    fresh = isinstance(s_all, jax.ShapeDtypeStruct)
    m = q.shape[0]
    _, b, h, hk, hv = s_all.shape
    nt = rows_per_batch // tile
    assert m == b * rows_per_batch and tile % chunk == 0 and chunk % GLA_SUB == 0
    kspec = pl.BlockSpec((tile, h * hk), lambda bi, ti: (bi * nt + ti, 0))
    vspec = pl.BlockSpec((tile, h * hv), lambda bi, ti: (bi * nt + ti, 0))
    sspec = pl.BlockSpec((None, None, h, hk, hv), lambda bi, ti: (layer, bi, 0, 0, 0))
    in_specs = [kspec, kspec, kspec, vspec, vspec, pl.BlockSpec((1, hv), lambda bi, ti: (0, 0))]
    args = [q, k, g, v, r, ng]
    if s0_all is not None:
        in_specs.append(sspec)
        args.append(s0_all)
    aliases = {}
    if not fresh:
        aliases = {len(args): 1}
        in_specs.append(pl.BlockSpec(memory_space=pl.ANY))
        args.append(s_all)
    return pl.pallas_call(
        functools.partial(_gla_kernel, chunk=chunk, nchunks=tile // chunk, nt=nt, heads=h, hk=hk, hv=hv,
                          has_s0=s0_all is not None),
        grid=(b, nt),
        in_specs=in_specs,
        out_specs=[vspec, sspec],
        out_shape=[jax.ShapeDtypeStruct((m, h * hv), BF16), jax.ShapeDtypeStruct(s_all.shape, F32)],
        scratch_shapes=[pltpu.VMEM((h, hv, hk), F32)],
        input_output_aliases=aliases,
        compiler_params=_cparams(("parallel", "arbitrary")),
        name="gla",
    )(*args)


def _lambda(lam_ref, lam_init):
    lf = lam_ref[...]
    a = jnp.sum(lf[0:1] * lf[1:2], axis=-1, keepdims=True)
    b = jnp.sum(lf[2:3] * lf[3:4], axis=-1, keepdims=True)
    return jnp.exp(a) - jnp.exp(b) + lam_init


def _attn_prompt_kernel(q_ref, kt_ref, v_ref, ng_ref, lam_ref, o_ref, qm_scr, vx_scr, m_scr, acc_scr,
                        *, tq, tk, td, tr, hd, lam_init):
    qi = pl.program_id(2)
    w = 2 * hd

    @pl.when(qi == 0)
    def _():
        vx_scr[:, 0:w] = v_ref[...]
        vx_scr[:, w:2 * w] = jnp.ones((v_ref.shape[0], w), BF16)

    q = q_ref[...]
    lane = lax.broadcasted_iota(jnp.int32, q.shape, 1)
    zero = jnp.zeros_like(q)
    qm_scr[0] = jnp.where(lane < hd, q, zero)
    qm_scr[1] = jnp.where(lane >= hd, q, zero)
    m_scr[...] = jnp.full(m_scr.shape, -jnp.inf, F32)
    acc_scr[...] = jnp.zeros(acc_scr.shape, F32)

    def absorb(cols, kw, c, r0, s, diag_row):
        rows = slice(r0, r0 + tr)
        if diag_row is not None and r0 < diag_row + kw:
            r_i = lax.broadcasted_iota(jnp.int32, s.shape, 0) + (r0 - diag_row)
            c_i = lax.broadcasted_iota(jnp.int32, s.shape, 1)
            s = jnp.where(r_i >= c_i, s, -jnp.inf)
        m_prev = m_scr[c, rows]
        m_new = jnp.maximum(m_prev, jnp.max(s, axis=-1, keepdims=True))
        alpha = jnp.exp2(m_prev - m_new)
        p = jnp.exp2(s - jnp.tile(m_new, (1, kw // LANES)))
        acc_scr[c, rows] = (jnp.tile(alpha, (1, 2)) * acc_scr[c, rows]
                            + _dot(p.astype(BF16), vx_scr[cols, :]))
        m_scr[c, rows] = m_new

    def run(tiles):
        work = []
        for k0, kw, diag_row in tiles:
            cols = pl.ds(pl.multiple_of(k0, kw), kw)
            kb = kt_ref[:, cols]
            for r0 in range(diag_row or 0, tq, tr):
                for c in range(2):
                    work.append((cols, kw, c, r0, _dot(qm_scr[c, r0:r0 + tr], kb), diag_row))
        for item in work:
            absorb(*item)

    def body(j, carry):
        run([(j * tk, tk, None)])
        return carry

    lax.fori_loop(0, qi * (tq // tk), body, 0)
    run([(qi * tq + dblk * td, td, dblk * td) for dblk in range(tq // td)])

    lam = _lambda(lam_ref, lam_init)
    a0 = acc_scr[0]
    a1 = acc_scr[1]
    o = a0[:, 0:w] / a0[:, w:2 * w] - lam * (a1[:, 0:w] / a1[:, w:2 * w])
    o_ref[...] = (_rms_norm(o, ng_ref[...]) * (1.0 - lam_init)).astype(BF16)


def _attn_prompt(q, kt, v, ng, lam_p, *, batch, heads, hd, seq, tq, tk, td, tr, lam_init):
    m, w = q.shape
    nq = seq // tq
    assert 2 * hd == LANES and tq % tk == 0 and tq % td == 0 and td % tr == 0
    qspec = pl.BlockSpec((tq, 2 * hd), lambda b, h, i: (b * nq + i, h))
    return pl.pallas_call(
        functools.partial(_attn_prompt_kernel, tq=tq, tk=tk, td=td, tr=tr, hd=hd, lam_init=lam_init),
        grid=(batch, heads, nq),
        in_specs=[qspec,
                  pl.BlockSpec((None, 2 * hd, seq), lambda b, h, i: (b, h, 0)),
                  pl.BlockSpec((seq, 2 * hd), lambda b, h, i: (b, h)),
                  pl.BlockSpec((1, 2 * hd), lambda b, h, i: (0, 0)),
                  pl.BlockSpec((4, hd), lambda b, h, i: (0, 0))],
        out_specs=qspec,
        out_shape=jax.ShapeDtypeStruct((m, w), BF16),
        scratch_shapes=[pltpu.VMEM((2, tq, 2 * hd), BF16), pltpu.VMEM((seq, 4 * hd), BF16),
                        pltpu.VMEM((2, tq, 2 * hd), F32), pltpu.VMEM((2, tq, 4 * hd), F32)],
        compiler_params=_cparams(("parallel", "parallel", "arbitrary")),
        name="attn_prompt",
    )(q, kt, v, ng, lam_p)


def _attn_decode_kernel(pt_ref, qbd_ref, kn_ref, vn_ref, ng_ref, lam_ref, *refs, pg, dec_seq, heads, hd, lam_init):
    del pt_ref
    k_refs = refs[:pg]
    v_refs = refs[pg:2 * pg]
    o_ref, m_scr, l_scr, acc_scr = refs[2 * pg:]
    j = pl.program_id(1)
    nj = pl.num_programs(1)
    qbd = qbd_ref[...]
    grp = 2 * dec_seq
    page = kn_ref.shape[-1]

    def pv(p, v_ref):
        outs = []
        for h in range(heads):
            v_h = v_ref[pl.ds(h, page, stride=heads), :].astype(BF16)
            outs.append(_dot(p[h * grp:(h + 1) * grp], v_h))
        return jnp.concatenate(outs, axis=0)

    @pl.when(j == 0)
    def _():
        s = _dot(qbd, kn_ref[...])
        r_i = lax.broadcasted_iota(jnp.int32, s.shape, 0) % dec_seq
        c_i = lax.broadcasted_iota(jnp.int32, s.shape, 1)
        s = jnp.where(c_i <= r_i, s, -jnp.inf)
        m0 = jnp.max(s, axis=-1, keepdims=True)
        p = jnp.exp2(s - m0)
        m_scr[...] = jnp.broadcast_to(m0, m_scr.shape)
        l_scr[...] = jnp.broadcast_to(jnp.sum(p, axis=-1, keepdims=True), l_scr.shape)
        acc_scr[...] = pv(p.astype(BF16), vn_ref)

    s = jnp.concatenate([_dot(qbd, kr[...].astype(BF16)) for kr in k_refs], axis=-1)
    m_prev = m_scr[...]
    m_new = jnp.maximum(m_prev, jnp.max(s, axis=-1, keepdims=True))
    alpha = jnp.exp2(m_prev - m_new)
    p = jnp.exp2(s - jnp.tile(m_new, (1, s.shape[1] // LANES)))
    l_scr[...] = alpha * l_scr[...] + jnp.sum(p, axis=-1, keepdims=True)
    pb = p.astype(BF16)
    upd = pv(pb[:, 0:page], v_refs[0])
    for i in range(1, pg):
        upd = upd + pv(pb[:, i * page:(i + 1) * page], v_refs[i])
    acc_scr[...] = alpha * acc_scr[...] + upd
    m_scr[...] = m_new

    @pl.when(j == nj - 1)
    def _():
        lam = _lambda(lam_ref, lam_init)
        ng = ng_ref[...]
        n = acc_scr[...] / l_scr[...]
        for h in range(heads):
            blk = n[h * grp:(h + 1) * grp]
            o = blk[0:dec_seq] - lam * blk[dec_seq:grp]
            o_ref[:, h * 2 * hd:(h + 1) * 2 * hd] = _rms_norm(o, ng) * (1.0 - lam_init)


def _attn_decode(page_table, qbd, kt_new, v_new, ng, lam_p, cache_kt, cache_v2, layer, *, pg, dec_seq, heads, hd,
                 lam_init):
    db, r, w = qbd.shape
    n_pages = page_table.shape[1]
    page = cache_kt.shape[-1]
    assert n_pages % pg == 0 and page == LANES and 2 * hd == LANES

    def kspec(i):
        return pl.BlockSpec((None, None, w, page), lambda b, j, pt, i=i: (layer, pt[b, j * pg + i], 0, 0))

    def vspec(i):
        return pl.BlockSpec((None, None, page * heads, 2 * hd),
                            lambda b, j, pt, i=i: (layer, pt[b, j * pg + i], 0, 0))

    in_specs = [pl.BlockSpec((None, r, w), lambda b, j, pt: (b, 0, 0)),
                pl.BlockSpec((None, w, page), lambda b, j, pt: (b, 0, 0)),
                pl.BlockSpec((None, page * heads, 2 * hd), lambda b, j, pt: (b, 0, 0)),
                pl.BlockSpec((1, 2 * hd), lambda b, j, pt: (0, 0)),
                pl.BlockSpec((4, hd), lambda b, j, pt: (0, 0))]
    in_specs += [kspec(i) for i in range(pg)] + [vspec(i) for i in range(pg)]
    return pl.pallas_call(
        functools.partial(_attn_decode_kernel, pg=pg, dec_seq=dec_seq, heads=heads, hd=hd, lam_init=lam_init),
        grid_spec=pltpu.PrefetchScalarGridSpec(
            num_scalar_prefetch=1,
            grid=(db, n_pages // pg),
            in_specs=in_specs,
            out_specs=pl.BlockSpec((None, dec_seq, w), lambda b, j, pt: (b, 0, 0)),
            scratch_shapes=[pltpu.VMEM((r, 2 * hd), F32)] * 3),
        out_shape=jax.ShapeDtypeStruct((db, dec_seq, w), F32),
        compiler_params=_cparams(("parallel", "arbitrary")),
        name="attn_decode",
    )(page_table, qbd, kt_new, v_new, ng, lam_p, *([cache_kt] * pg), *([cache_v2] * pg))


def _mix_kernel(x_ref, sh_ref, sc_ref, gt_ref, oa_ref, ob_ref, wga_ref, wgb_ref, wa_ref, wb_ref, wo_ref,
                g_ref, b_ref, o_ref, *, alpha):
    x = x_ref[...]
    u = (x * (1.0 + sc_ref[...]) + sh_ref[...]).astype(BF16)
    ya = _sigmoid(_dot(u, wga_ref[...])) * _dot(oa_ref[...], wa_ref[...])
    yb = _sigmoid(_dot(u, wgb_ref[...])) * _dot(ob_ref[...], wb_ref[...])
    z = _dot((ya + yb).astype(BF16), wo_ref[...])
    o_ref[...] = _layer_norm(alpha * x + gt_ref[...] * z, g_ref[...], b_ref[...])


def _mix(x, mod, layer, oa, ob, wgate, wa, wb, wo, ln_g, ln_b, *, alpha, tm, rows_per_batch):
    m, d = x.shape
    row = pl.BlockSpec((tm, d), lambda i: (i, 0))
    wsq = _resident((None, d, d), lambda i: (layer, 0, 0))
    in_specs = [row] + _mod_specs(mod, layer, 1, tm, rows_per_batch, (0, 1, 2)) + [
        row, row,
        _resident((None, d, d), lambda i: (layer, 0, 0)),
        _resident((None, d, d), lambda i: (layer, 0, 1)),
        wsq, wsq, wsq,
        pl.BlockSpec((None, None, 1, d), lambda i: (layer, 1, 0, 0)),
        pl.BlockSpec((None, None, 1, d), lambda i: (layer, 1, 0, 0)),
    ]
    return pl.pallas_call(
        functools.partial(_mix_kernel, alpha=alpha),
        grid=(m // tm,),
        in_specs=in_specs,
        out_specs=row,
        out_shape=jax.ShapeDtypeStruct((m, d), F32),
        compiler_params=_cparams(("parallel",)),
        name="mix",
    )(x, mod, mod, mod, oa, ob, wgate, wgate, wa, wb, wo, ln_g, ln_b)


def _rope_angles(pos, hd):
    half = hd // 2
    inv = ROPE_THETA ** (-jnp.arange(half, dtype=F32) * (2.0 / hd))
    ang = pos.astype(F32)[:, None] * inv[None, :]
    return jnp.cos(ang), jnp.sin(ang)


def _rope_lane_tables(cos, sin, hd):
    reps = LANES // hd
    return (jnp.tile(jnp.concatenate([cos, cos], -1), (1, reps)),
            jnp.tile(jnp.concatenate([-sin, sin], -1), (1, reps)))


def _tile(n, pref):
    t = min(n, pref)
    while n % t:
        t //= 2
    return t


MXU_WIDTH = 256


def _mxu_chunk(n):
    return MXU_WIDTH if n % MXU_WIDTH == 0 else LANES


def kernel(x_prompt, x_sample, cache_k, cache_v, state_gla, page_table, c_prompt, c_sample, w_ada, b_ada,
           ln_g, ln_b, ffn_w_in, ffn_w_out, w_in, gla_w_f2, gla_b_f2, gla_norm_g, diff_lam, diff_norm_g,
           w_branch_a, w_branch_b, w_out):
    batch, seq, d = x_prompt.shape
    db, dec_seq, _ = x_sample.shape
    depth = w_ada.shape[0]
    gla_heads, hk, hv = state_gla.shape[2:]
    n_pool, page, heads, _, hd = cache_k.shape[1:]
    dk, dv = gla_heads * hk, gla_heads * hv
    wq = 2 * heads * hd
    rank = gla_w_f2.shape[1]
    f = ffn_w_out.shape[2]
    past = page_table.shape[1] * page
    alpha = (2.0 * depth) ** 0.25
    mp, ms = batch * seq, db * dec_seq

    o_f = 2 * dk + 2 * dv
    o_dq = o_f + rank
    o_ga = o_dq + 3 * wq
    wg = w_in[:, :, 0:o_f].astype(BF16)
    rank_pad = -(-rank // LANES) * LANES
    wf1 = jnp.pad(w_in[:, :, o_f:o_dq], ((0, 0), (0, 0), (0, rank_pad - rank))).astype(BF16)
    wf2 = jnp.pad(gla_w_f2, ((0, 0), (0, rank_pad - rank), (0, 0))).astype(BF16)
    bf2 = gla_b_f2.reshape(depth, 1, dk)
    wd = w_in[:, :, o_dq:o_ga].astype(BF16)
    wkt = jnp.swapaxes(w_in[:, :, o_dq + wq:o_dq + 2 * wq], 1, 2).astype(BF16)
    wgate = w_in[:, :, o_ga:o_ga + 2 * d].astype(BF16)
    ffn_in = ffn_w_in.astype(BF16)
    ffn_out = ffn_w_out.astype(BF16)
    wa = w_branch_a.astype(BF16)
    wb = w_branch_b.astype(BF16)
    wo = w_out.astype(BF16)
    ln_g4 = ln_g.reshape(depth, 3, 1, d)
    ln_b4 = ln_b.reshape(depth, 3, 1, d)
    gla_ng = gla_norm_g.reshape(depth, 1, hv)
    diff_ng = diff_norm_g.reshape(depth, 1, 2 * hd)

    n_c = batch + db
    n_c_pad = -(-n_c // 8) * 8
    c_all = jnp.pad(jnp.concatenate([c_prompt, c_sample], 0), ((0, n_c_pad - n_c), (0, 0)))
    mod = _ada_mod(c_all, w_ada, b_ada)
    mod_p = mod[:, :batch].reshape(depth, batch, 1, N_MOD * d)
    mod_s = jnp.repeat(mod[:, batch:n_c], dec_seq, axis=1)

    cos_p, sin_p = _rope_angles(jnp.arange(seq, dtype=jnp.int32), hd)
    cos_pl, sin_pl = _rope_lane_tables(cos_p, sin_p, hd)
    cos_pt, sin_pt = cos_p.T, sin_p.T
    cos_s, sin_s = _rope_lane_tables(*_rope_angles(past + jnp.arange(dec_seq, dtype=jnp.int32), hd), hd)
    cos_s = jnp.tile(cos_s, (db, 1))
    sin_s = jnp.tile(sin_s, (db, 1))

    tm_p = _tile(seq, 512)
    fc = _mxu_chunk(f)
    tq = _tile(seq, 2048)
    tk = _tile(tq, 1024)
    td = _tile(tq, 512)
    tr = td
    gla_tile = _tile(seq, 512)
    gla_chunk = _tile(gla_tile, 64)
    dec_pad = -(-dec_seq // BF16_ROWS) * BF16_ROWS
    pg = _tile(page_table.shape[1], 16)
    cache_kt = cache_k.transpose(0, 1, 3, 4, 5, 2).reshape(depth, n_pool, wq, page)
    cache_v2 = cache_v.reshape(depth, n_pool, page * heads, 2 * hd)
    eye_hc = jnp.eye(2 * heads, dtype=BF16)

    hp = x_prompt.reshape(mp, d)
    hs = x_sample.reshape(ms, d)
    k_all = jax.ShapeDtypeStruct((depth, batch, wq, seq), F32)
    v_all = jax.ShapeDtypeStruct((depth, mp * heads, 2 * hd), F32)
    sp_all = jax.ShapeDtypeStruct((depth, batch, gla_heads, hk, hv), F32)
    sd_all = jax.ShapeDtypeStruct((depth, db, gla_heads, hk, hv), F32)
    kd, vd = [], []

    def pad_rows(t):
        t3 = t.reshape(db, dec_seq, t.shape[-1])
        return jnp.pad(t3, ((0, 0), (0, dec_pad - dec_seq), (0, 0))).reshape(db * dec_pad, t.shape[-1])

    for l in range(depth):
        lam_init = 0.8 - 0.6 * math.exp(-0.3 * l)

        hp = _ffn(hp, mod_p, l, 0, 0, ffn_in, ffn_out, ln_g4, ln_b4, alpha=alpha, tm=tm_p, fc=fc,
                  rows_per_batch=seq)
        gq, gk, gv, gr, gg = _proj_gla(hp, mod_p, l, wg, wf1, wf2, bf2, dk=dk, dv=dv, hk=hk, tm=tm_p,
                                       rows_per_batch=seq)
        k_all, v_all, dq, ktb, vb = _proj_diff_cols(hp, mod_p, l, wd, wkt, cos_pl, sin_pl, cos_pt, sin_pt,
                                                    k_all, v_all, hd=hd, heads=heads, tm=tm_p, seq=seq)
        oa, sp_all = _gla(gq, gk, gg, gv, gr, gla_ng[l], sp_all, l, rows_per_batch=seq, tile=gla_tile,
                          chunk=gla_chunk)
        ob = _attn_prompt(dq, ktb, vb, diff_ng[l], diff_lam[l], batch=batch, heads=heads, hd=hd, seq=seq,
                          tq=tq, tk=tk, td=td, tr=tr, lam_init=lam_init)
        hp = _mix(hp, mod_p, l, oa, ob, wgate, wa, wb, wo, ln_g4, ln_b4, alpha=alpha, tm=tm_p,
                  rows_per_batch=seq)
        hp = _ffn(hp, mod_p, l, 2, 1, ffn_in, ffn_out, ln_g4, ln_b4, alpha=alpha, tm=tm_p, fc=fc,
                  rows_per_batch=seq)

        hs = _ffn(hs, mod_s, l, 0, 0, ffn_in, ffn_out, ln_g4, ln_b4, alpha=alpha, tm=ms, fc=fc,
                  rows_per_batch=None)
        gq, gk, gv, gr, gg = _proj_gla(hs, mod_s, l, wg, wf1, wf2, bf2, dk=dk, dv=dv, hk=hk, tm=ms,
                                       rows_per_batch=None)
        dq, kf, kb, vf = _proj_diff_rows(hs, mod_s, l, wd, cos_s, sin_s, hd=hd, tm=ms)
        oa, sd_all = _gla(pad_rows(gq), pad_rows(gk), pad_rows(gg), pad_rows(gv), pad_rows(gr), gla_ng[l],
                          sd_all, l, state_gla, rows_per_batch=dec_pad, tile=dec_pad, chunk=dec_pad)
        oa = oa.reshape(db, dec_pad, dv)[:, :dec_seq].reshape(ms, dv)
        q5 = dq.reshape(db, dec_seq, 2 * heads, hd)
        qbd = (q5.transpose(0, 2, 1, 3)[:, :, :, None, :] * eye_hc[None, :, None, :, None]).reshape(
            db, 2 * heads * dec_seq, wq)
        kt_new = jnp.pad(kb.reshape(db, dec_seq, wq).transpose(0, 2, 1), ((0, 0), (0, 0), (0, page - dec_seq)))
        v_new = jnp.pad(vf.reshape(db, dec_seq, wq), ((0, 0), (0, page - dec_seq), (0, 0))).reshape(
            db, page * heads, 2 * hd)
        ob = _attn_decode(page_table, qbd, kt_new, v_new, diff_ng[l], diff_lam[l], cache_kt, cache_v2, l, pg=pg,
                          dec_seq=dec_seq, heads=heads, hd=hd, lam_init=lam_init)
        ob = ob.reshape(ms, wq).astype(BF16)
        hs = _mix(hs, mod_s, l, oa, ob, wgate, wa, wb, wo, ln_g4, ln_b4, alpha=alpha, tm=ms,
                  rows_per_batch=None)
        hs = _ffn(hs, mod_s, l, 2, 1, ffn_in, ffn_out, ln_g4, ln_b4, alpha=alpha, tm=ms, fc=fc,
                  rows_per_batch=None)
        kd.append(kf)
        vd.append(vf)

    k_prompt = k_all.reshape(depth, batch, heads, 2, hd, seq).transpose(0, 1, 5, 2, 3, 4)
    return (hp.reshape(batch, seq, d),
            hs.reshape(db, dec_seq, d),
            k_prompt,
            v_all.reshape(depth, batch, seq, heads, 2 * hd),
            sp_all,
            jnp.stack(kd).reshape(depth, db, dec_seq, heads, 2, hd),
            jnp.stack(vd).reshape(depth, db, dec_seq, heads, 2 * hd),
            sd_all)
```

```python
import functools
import math

import jax
import jax.numpy as jnp
from jax import lax
from jax.experimental import pallas as pl
from jax.experimental.pallas import tpu as pltpu

F32 = jnp.float32
BF16 = jnp.bfloat16

LN_EPS = 1e-5
GLA_TAU = 16.0
ROPE_THETA = 10000.0
N_MOD = 9
LANES = 128
GLA_SUB = 8
BF16_ROWS = 16
VMEM_LIMIT_BYTES = 56 * 1024 * 1024
LOG2E = math.log2(math.e)


def _cparams(sem):
    return pltpu.CompilerParams(dimension_semantics=sem, vmem_limit_bytes=VMEM_LIMIT_BYTES)


def _dot(a, b):
    return jnp.dot(a, b, preferred_element_type=F32)


def _dot_nt(a, b):
    return lax.dot_general(a, b, (((1,), (1,)), ((), ())), preferred_element_type=F32)


def _dot_tn(a, b):
    return lax.dot_general(a, b, (((0,), (0,)), ((), ())), preferred_element_type=F32)


def _sigmoid(x):
    return 1.0 / (1.0 + jnp.exp(-x))


def _silu(x):
    return x * _sigmoid(x)


def _layer_norm(y, g, b):
    mu = jnp.mean(y, axis=-1, keepdims=True)
    yc = y - mu
    var = jnp.mean(yc * yc, axis=-1, keepdims=True)
    return yc * lax.rsqrt(var + LN_EPS) * g + b


def _rms_norm(o, g):
    return o * lax.rsqrt(jnp.mean(o * o, axis=-1, keepdims=True) + LN_EPS) * g


def _modulated(x_ref, sh_ref, sc_ref):
    return (x_ref[...] * (1.0 + sc_ref[...]) + sh_ref[...]).astype(BF16)


def _mod_specs(mod, layer, sub, tm, rows_per_batch, which):
    d = mod.shape[-1] // N_MOD
    specs = []
    for w in which:
        col = 3 * sub + w
        if rows_per_batch is not None:
            tiles = rows_per_batch // tm
            specs.append(pl.BlockSpec((None, None, 1, d),
                                      lambda i, *_, col=col, tiles=tiles: (layer, i // tiles, 0, col)))
        else:
            specs.append(pl.BlockSpec((None, tm, d), lambda i, *_, col=col: (layer, i, col)))
    return specs


def _ada_kernel(c_ref, w_ref, b_ref, o_ref):
    a = _silu(c_ref[...]).astype(BF16)
    o_ref[...] = _dot(a, w_ref[...].astype(BF16)) + b_ref[...]


def _ada_mod(c_all, w_ada, b_ada):
    depth, d, n = w_ada.shape
    r = c_all.shape[0]
    tn = d
    return pl.pallas_call(
        _ada_kernel,
        grid=(depth, n // tn),
        in_specs=[pl.BlockSpec((r, d), lambda l, j: (0, 0)),
                  pl.BlockSpec((None, d, tn), lambda l, j: (l, 0, j)),
                  pl.BlockSpec((None, 1, tn), lambda l, j: (l, 0, j))],
        out_specs=pl.BlockSpec((None, r, tn), lambda l, j: (l, 0, j)),
        out_shape=jax.ShapeDtypeStruct((depth, r, n), F32),
        compiler_params=_cparams(("parallel", "parallel")),
        name="ada_mod",
    )(c_all, w_ada, b_ada.reshape(depth, 1, n))


def _ffn_kernel(x_ref, sh_ref, sc_ref, gt_ref, wi_ref, wo_ref, g_ref, b_ref, o_ref, *, alpha, f, fc):
    x = x_ref[...]
    u = (x * (1.0 + sc_ref[...]) + sh_ref[...]).astype(BF16)

    def up(k):
        return (_dot(u, wi_ref[:, k * fc:(k + 1) * fc]), _dot(u, wi_ref[:, f + k * fc:f + (k + 1) * fc]))

    nk = f // fc
    cur = up(0)
    acc = None
    for k in range(nk):
        nxt = up(k + 1) if k + 1 < nk else None
        part = _dot((_silu(cur[0]) * cur[1]).astype(BF16), wo_ref[k * fc:(k + 1) * fc, :])
        acc = part if acc is None else acc + part
        cur = nxt
    y = alpha * x + gt_ref[...] * (0.5 * acc)
    o_ref[...] = _layer_norm(y, g_ref[...], b_ref[...])


def _resident(block_shape, index_map):
    return pl.BlockSpec(block_shape, index_map, pipeline_mode=pl.Buffered(1))


def _ffn(x, mod, layer, sub, which_ffn, w_in, w_out, ln_g, ln_b, *, alpha, tm, fc, rows_per_batch):
    m, d = x.shape
    f = w_out.shape[2]
    assert f % fc == 0
    in_specs = [pl.BlockSpec((tm, d), lambda i: (i, 0))]
    in_specs += _mod_specs(mod, layer, sub, tm, rows_per_batch, (0, 1, 2))
    in_specs += [
        _resident((None, None, d, 2 * f), lambda i: (layer, which_ffn, 0, 0)),
        _resident((None, None, f, d), lambda i: (layer, which_ffn, 0, 0)),
        pl.BlockSpec((None, None, 1, d), lambda i: (layer, sub, 0, 0)),
        pl.BlockSpec((None, None, 1, d), lambda i: (layer, sub, 0, 0)),
    ]
    return pl.pallas_call(
        functools.partial(_ffn_kernel, alpha=alpha, f=f, fc=fc),
        grid=(m // tm,),
        in_specs=in_specs,
        out_specs=pl.BlockSpec((tm, d), lambda i: (i, 0)),
        out_shape=jax.ShapeDtypeStruct((m, d), F32),
        compiler_params=_cparams(("parallel",)),
        name="ffn",
    )(x, mod, mod, mod, w_in, w_out, ln_g, ln_b)


def _proj_gla_kernel(x_ref, sh_ref, sc_ref, wg_ref, wf1_ref, wf2_ref, bf2_ref,
                     q_ref, k_ref, v_ref, r_ref, g_ref, *, dk, dv, qscale):
    u = _modulated(x_ref, sh_ref, sc_ref)
    q_ref[...] = _dot(u, wg_ref[:, 0:dk]) * qscale
    k_ref[...] = _dot(u, wg_ref[:, dk:2 * dk])
    v_ref[...] = _dot(u, wg_ref[:, 2 * dk:2 * dk + dv]).astype(BF16)
    r_ref[...] = _silu(_dot(u, wg_ref[:, 2 * dk + dv:2 * dk + 2 * dv]))
    gf = _dot(u, wf1_ref[...])
    z = _dot(gf.astype(BF16), wf2_ref[...]) + bf2_ref[...]
    g_ref[...] = (jnp.minimum(z, 0.0) - jnp.log1p(jnp.exp(-jnp.abs(z)))) * (1.0 / GLA_TAU)


def _proj_gla(x, mod, layer, wg, wf1, wf2, bf2, *, dk, dv, hk, tm, rows_per_batch):
    m, d = x.shape
    rp = wf1.shape[-1]
    in_specs = [pl.BlockSpec((tm, d), lambda i: (i, 0))]
    in_specs += _mod_specs(mod, layer, 1, tm, rows_per_batch, (0, 1))
    in_specs += [
        pl.BlockSpec((None, d, 2 * dk + 2 * dv), lambda i: (layer, 0, 0)),
        pl.BlockSpec((None, d, rp), lambda i: (layer, 0, 0)),
        pl.BlockSpec((None, rp, dk), lambda i: (layer, 0, 0)),
        pl.BlockSpec((None, 1, dk), lambda i: (layer, 0, 0)),
    ]
    out_shapes = [jax.ShapeDtypeStruct((m, dk), F32), jax.ShapeDtypeStruct((m, dk), F32),
                  jax.ShapeDtypeStruct((m, dv), BF16), jax.ShapeDtypeStruct((m, dv), F32),
                  jax.ShapeDtypeStruct((m, dk), F32)]
    return pl.pallas_call(
        functools.partial(_proj_gla_kernel, dk=dk, dv=dv, qscale=hk ** -0.5),
        grid=(m // tm,),
        in_specs=in_specs,
        out_specs=[pl.BlockSpec((tm, s.shape[1]), lambda i: (i, 0)) for s in out_shapes],
        out_shape=out_shapes,
        compiler_params=_cparams(("parallel",)),
        name="proj_gla",
    )(x, mod, mod, wg, wf1, wf2, bf2)


def _rope_lanes(t, cos, sin, first_half, hd):
    rot = jnp.where(first_half, pltpu.roll(t, LANES - hd // 2, 1), pltpu.roll(t, hd // 2, 1))
    return t * cos + rot * sin


def _proj_q(u, wd_ref, cos, sin, first_half, q_ref, w, hd, qscale):
    dq = _dot(u, wd_ref[:, 0:w])
    for c in range(w // LANES):
        sl = slice(c * LANES, (c + 1) * LANES)
        q_ref[:, sl] = (_rope_lanes(dq[:, sl], cos, sin, first_half, hd) * qscale).astype(BF16)


def _proj_diff_rows_kernel(x_ref, sh_ref, sc_ref, wd_ref, cos_ref, sin_ref,
                           q_ref, kf_ref, kb_ref, vf_ref, *, w, hd, qscale):
    u = _modulated(x_ref, sh_ref, sc_ref)
    cos = cos_ref[...]
    sin = sin_ref[...]
    first_half = (lax.broadcasted_iota(jnp.int32, cos.shape, 1) % hd) < (hd // 2)
    _proj_q(u, wd_ref, cos, sin, first_half, q_ref, w, hd, qscale)
    dk = _dot(u, wd_ref[:, w:2 * w])
    for c in range(w // LANES):
        sl = slice(c * LANES, (c + 1) * LANES)
        kr = _rope_lanes(dk[:, sl], cos, sin, first_half, hd)
        kf_ref[:, sl] = kr
        kb_ref[:, sl] = kr.astype(BF16)
    vf_ref[...] = _dot(u, wd_ref[:, 2 * w:3 * w])


def _proj_diff_cols_kernel(x_ref, sh_ref, sc_ref, wd_ref, wkt_ref, cos_ref, sin_ref, cost_ref, sint_ref, *refs,
                           w, hd, heads, qscale):
    kt_ref, vf_ref, q_ref, ktb_ref, vb_ref = refs[-5:]
    u = _modulated(x_ref, sh_ref, sc_ref)
    cos = cos_ref[...]
    sin = sin_ref[...]
    first_half = (lax.broadcasted_iota(jnp.int32, cos.shape, 1) % hd) < (hd // 2)
    _proj_q(u, wd_ref, cos, sin, first_half, q_ref, w, hd, qscale)
    dkt = _dot_nt(wkt_ref[...], u)
    cost = cost_ref[...]
    sint = sint_ref[...]
    half = hd // 2
    for g in range(w // hd):
        x1 = dkt[g * hd:g * hd + half]
        x2 = dkt[g * hd + half:(g + 1) * hd]
        r1 = x1 * cost - x2 * sint
        r2 = x2 * cost + x1 * sint
        kt_ref[g * hd:g * hd + half, :] = r1
        kt_ref[g * hd + half:(g + 1) * hd, :] = r2
        ktb_ref[g * hd:g * hd + half, :] = r1.astype(BF16)
        ktb_ref[g * hd + half:(g + 1) * hd, :] = r2.astype(BF16)
    dv = _dot(u, wd_ref[:, 2 * w:3 * w])
    tm = dv.shape[0]
    for h in range(heads):
        vf_ref[pl.ds(h, tm, stride=heads), :] = dv[:, h * 2 * hd:(h + 1) * 2 * hd]
    vb_ref[...] = dv.astype(BF16)


def _proj_diff_rows(x, mod, layer, wd, cos_t, sin_t, *, hd, tm):
    m, d = x.shape
    w = wd.shape[-1] // 3
    in_specs = [pl.BlockSpec((tm, d), lambda i: (i, 0))]
    in_specs += _mod_specs(mod, layer, 1, tm, None, (0, 1))
    in_specs += [
        pl.BlockSpec((None, d, 3 * w), lambda i: (layer, 0, 0)),
        pl.BlockSpec((tm, LANES), lambda i: (i, 0)),
        pl.BlockSpec((tm, LANES), lambda i: (i, 0)),
    ]
    out_shapes = [jax.ShapeDtypeStruct((m, w), BF16), jax.ShapeDtypeStruct((m, w), F32),
                  jax.ShapeDtypeStruct((m, w), BF16), jax.ShapeDtypeStruct((m, w), F32)]
    return pl.pallas_call(
        functools.partial(_proj_diff_rows_kernel, w=w, hd=hd, qscale=hd ** -0.5 * LOG2E),
        grid=(m // tm,),
        in_specs=in_specs,
        out_specs=[pl.BlockSpec((tm, w), lambda i: (i, 0)) for _ in out_shapes],
        out_shape=out_shapes,
        compiler_params=_cparams(("parallel",)),
        name="proj_diff_rows",
    )(x, mod, mod, wd, cos_t, sin_t)


def _proj_diff_cols(x, mod, layer, wd, wkt, cos_t, sin_t, cos_tt, sin_tt, k_all, v_all, *, hd, heads, tm, seq):
    m, d = x.shape
    w = wkt.shape[1]
    npos = seq // tm
    batch = m // seq
    in_specs = [pl.BlockSpec((tm, d), lambda i: (i, 0))]
    in_specs += _mod_specs(mod, layer, 1, tm, seq, (0, 1))
    in_specs += [
        pl.BlockSpec((None, d, 3 * w), lambda i: (layer, 0, 0)),
        pl.BlockSpec((None, w, d), lambda i: (layer, 0, 0)),
        pl.BlockSpec((tm, LANES), lambda i: (i % npos, 0)),
        pl.BlockSpec((tm, LANES), lambda i: (i % npos, 0)),
        pl.BlockSpec((hd // 2, tm), lambda i: (0, i % npos)),
        pl.BlockSpec((hd // 2, tm), lambda i: (0, i % npos)),
    ]
    args = [x, mod, mod, wd, wkt, cos_t, sin_t, cos_tt, sin_tt]
    aliases = {len(args): 0, len(args) + 1: 1}
    in_specs += [pl.BlockSpec(memory_space=pl.ANY), pl.BlockSpec(memory_space=pl.ANY)]
    args += [k_all, v_all]
    out_shapes = [jax.ShapeDtypeStruct(k_all.shape, F32), jax.ShapeDtypeStruct(v_all.shape, F32),
                  jax.ShapeDtypeStruct((m, w), BF16),
                  jax.ShapeDtypeStruct((batch, w, seq), BF16), jax.ShapeDtypeStruct((m, w), BF16)]
    out_specs = [
        pl.BlockSpec((None, None, w, tm), lambda i: (layer, i // npos, 0, i % npos)),
        pl.BlockSpec((None, tm * heads, 2 * hd), lambda i: (layer, i, 0)),
        pl.BlockSpec((tm, w), lambda i: (i, 0)),
        pl.BlockSpec((None, w, tm), lambda i: (i // npos, 0, i % npos)),
        pl.BlockSpec((tm, w), lambda i: (i, 0)),
    ]
    return pl.pallas_call(
        functools.partial(_proj_diff_cols_kernel, w=w, hd=hd, heads=heads, qscale=hd ** -0.5 * LOG2E),
        grid=(m // tm,),
        in_specs=in_specs,
        out_specs=out_specs,
        out_shape=out_shapes,
        input_output_aliases=aliases,
        compiler_params=_cparams(("parallel",)),
        name="proj_diff_cols",
    )(*args)


def _gla_chunk(qs, ks, g_all, vs, rs, sts, ng, tril, causal, sub_lane):
    heads = len(qs)
    c_rows, hk = qs[0].shape
    nsub = c_rows // GLA_SUB
    cum_all = jnp.dot(tril, g_all * LOG2E, precision=lax.Precision.HIGHEST, preferred_element_type=F32)
    cums = [cum_all[:, h * hk:(h + 1) * hk] for h in range(heads)]
    lasts = [cum[c_rows - 1:c_rows] for cum in cums]

    inter = [_dot_nt((q * jnp.exp2(cum)).astype(BF16), st.astype(BF16)) for q, cum, st in zip(qs, cums, sts)]
    offs = []
    for q, k, cum in zip(qs, ks, cums):
        per_block = [None]
        for i in range(1, nsub):
            lo = i * GLA_SUB
            base = cum[lo - 1:lo]
            qt = (q[lo:lo + GLA_SUB] * jnp.exp2(cum[lo:lo + GLA_SUB] - base)).astype(BF16)
            kt = jnp.concatenate([(k[0:lo] * jnp.exp2(base - cum[0:lo])).astype(F32),
                                  jnp.zeros((c_rows - lo, hk), F32)], axis=0).astype(BF16)
            per_block.append(_dot_nt(qt, kt))
        offs.append(per_block)
    new_sts = [st * jnp.exp2(last) + _dot_tn(v, (k * jnp.exp2(last - cum)).astype(BF16))
               for k, v, cum, last, st in zip(ks, vs, cums, lasts, sts)]

    atts = []
    for q, k, cum, off in zip(qs, ks, cums, offs):
        blocks = []
        for i in range(nsub):
            lo = i * GLA_SUB
            qi = q[lo:lo + GLA_SUB]
            ci = cum[lo:lo + GLA_SUB]
            acc = jnp.zeros((GLA_SUB, c_rows), F32)
            for s in range(GLA_SUB):
                dec = jnp.exp2(jnp.minimum(ci - ci[s:s + 1], 0.0))
                a_col = jnp.sum(qi * k[lo + s:lo + s + 1] * dec, axis=-1, keepdims=True)
                acc = jnp.where(sub_lane == lo + s, a_col, acc)
            if i > 0:
                acc = jnp.where(sub_lane < lo, off[i], acc)
            blocks.append(acc)
        att = blocks[0] if nsub == 1 else jnp.concatenate(blocks, axis=0)
        atts.append(jnp.where(causal, att, 0.0).astype(BF16))

    outs = [(_rms_norm(o + _dot(att, v), ng) * r).astype(BF16) for o, att, v, r in zip(inter, atts, vs, rs)]
    return outs, new_sts


def _gla_kernel(*refs, chunk, nchunks, nt, heads, hk, hv, has_s0):
    q_ref, k_ref, g_ref, v_ref, r_ref, ng_ref = refs[:6]
    s0_ref = refs[6] if has_s0 else None
    o_ref, sout_ref, st_scr = refs[-3:]
    t = pl.program_id(1)

    @pl.when(t == 0)
    def _():
        for h in range(heads):
            st_scr[h] = s0_ref[h].T if has_s0 else jnp.zeros((hv, hk), F32)

    row = lax.broadcasted_iota(jnp.int32, (chunk, chunk), 0)
    col = lax.broadcasted_iota(jnp.int32, (chunk, chunk), 1)
    causal = row >= col
    tril = causal.astype(F32)
    sub_lane = lax.broadcasted_iota(jnp.int32, (GLA_SUB, chunk), 1)
    ng = ng_ref[...]

    def body(c, carry):
        rows = pl.ds(pl.multiple_of(c * chunk, chunk), chunk)
        kcs = [slice(h * hk, (h + 1) * hk) for h in range(heads)]
        vcs = [slice(h * hv, (h + 1) * hv) for h in range(heads)]
        outs, new_sts = _gla_chunk([q_ref[rows, kc] for kc in kcs], [k_ref[rows, kc] for kc in kcs],
                                   g_ref[rows, :], [v_ref[rows, vc] for vc in vcs],
                                   [r_ref[rows, vc] for vc in vcs], [st_scr[h] for h in range(heads)],
                                   ng, tril, causal, sub_lane)
        for h in range(heads):
            o_ref[rows, vcs[h]] = outs[h]
            st_scr[h] = new_sts[h]
        return carry

    lax.fori_loop(0, nchunks, body, 0)

    @pl.when(t == nt - 1)
    def _():
        for h in range(heads):
            sout_ref[h] = st_scr[h].T


def _gla(q, k, g, v, r, ng, s_all, layer, s0_all=None, *, rows_per_batch, tile, chunk):
    m = q.shape[0]
    _, b, h, hk, hv = s_all.shape
    nt = rows_per_batch // tile
    assert m == b * rows_per_batch and tile % chunk == 0 and chunk % GLA_SUB == 0
    kspec = pl.BlockSpec((tile, h * hk), lambda bi, ti: (bi * nt + ti, 0))
    vspec = pl.BlockSpec((tile, h * hv), lambda bi, ti: (bi * nt + ti, 0))
    sspec = pl.BlockSpec((None, None, h, hk, hv), lambda bi, ti: (layer, bi, 0, 0, 0))
    in_specs = [kspec, kspec, kspec, vspec, vspec, pl.BlockSpec((1, hv), lambda bi, ti: (0, 0))]
    args = [q, k, g, v, r, ng]
    if s0_all is not None:
        in_specs.append(sspec)
        args.append(s0_all)
    aliases = {len(args): 1}
    in_specs.append(pl.BlockSpec(memory_space=pl.ANY))
    args.append(s_all)
    return pl.pallas_call(
        functools.partial(_gla_kernel, chunk=chunk, nchunks=tile // chunk, nt=nt, heads=h, hk=hk, hv=hv,
                          has_s0=s0_all is not None),
        grid=(b, nt),
        in_specs=in_specs,
        out_specs=[vspec, sspec],
        out_shape=[jax.ShapeDtypeStruct((m, h * hv), BF16), jax.ShapeDtypeStruct(s_all.shape, F32)],
        scratch_shapes=[pltpu.VMEM((h, hv, hk), F32)],
        input_output_aliases=aliases,
        compiler_params=_cparams(("parallel", "arbitrary")),
        name="gla",
    )(*args)


def _lambda(lam_ref, lam_init):
    lf = lam_ref[...]
    a = jnp.sum(lf[0:1] * lf[1:2], axis=-1, keepdims=True)
    b = jnp.sum(lf[2:3] * lf[3:4], axis=-1, keepdims=True)
    return jnp.exp(a) - jnp.exp(b) + lam_init


def _attn_prompt_kernel(q_ref, kt_ref, v_ref, ng_ref, lam_ref, o_ref, qm_scr, vx_scr, m_scr, acc_scr,
                        *, tq, tk, td, tr, hd, lam_init):
    qi = pl.program_id(2)
    w = 2 * hd

    @pl.when(qi == 0)
    def _():
        vx_scr[:, 0:w] = v_ref[...]
        vx_scr[:, w:2 * w] = jnp.ones((v_ref.shape[0], w), BF16)

    q = q_ref[...]
    lane = lax.broadcasted_iota(jnp.int32, q.shape, 1)
    zero = jnp.zeros_like(q)
    qm_scr[0] = jnp.where(lane < hd, q, zero)
    qm_scr[1] = jnp.where(lane >= hd, q, zero)
    m_scr[...] = jnp.full(m_scr.shape, -jnp.inf, F32)
    acc_scr[...] = jnp.zeros(acc_scr.shape, F32)

    def absorb(cols, kw, c, r0, s, diag_row):
        rows = slice(r0, r0 + tr)
        if diag_row is not None and r0 < diag_row + kw:
            r_i = lax.broadcasted_iota(jnp.int32, s.shape, 0) + (r0 - diag_row)
            c_i = lax.broadcasted_iota(jnp.int32, s.shape, 1)
            s = jnp.where(r_i >= c_i, s, -jnp.inf)
        m_prev = m_scr[c, rows]
        m_new = jnp.maximum(m_prev, jnp.max(s, axis=-1, keepdims=True))
        alpha = jnp.exp2(m_prev - m_new)
        p = jnp.exp2(s - jnp.tile(m_new, (1, kw // LANES)))
        acc_scr[c, rows] = (jnp.tile(alpha, (1, 2)) * acc_scr[c, rows]
                            + _dot(p.astype(BF16), vx_scr[cols, :]))
        m_scr[c, rows] = m_new

    def run(tiles):
        work = []
        for k0, kw, diag_row in tiles:
            cols = pl.ds(pl.multiple_of(k0, kw), kw)
            kb = kt_ref[:, cols]
            for r0 in range(diag_row or 0, tq, tr):
                for c in range(2):
                    work.append((cols, kw, c, r0, _dot(qm_scr[c, r0:r0 + tr], kb), diag_row))
        for item in work:
            absorb(*item)

    def body(j, carry):
        run([(j * tk, tk, None)])
        return carry

    lax.fori_loop(0, qi * (tq // tk), body, 0)
    run([(qi * tq + dblk * td, td, dblk * td) for dblk in range(tq // td)])

    lam = _lambda(lam_ref, lam_init)
    a0 = acc_scr[0]
    a1 = acc_scr[1]
    o = a0[:, 0:w] / a0[:, w:2 * w] - lam * (a1[:, 0:w] / a1[:, w:2 * w])
    o_ref[...] = (_rms_norm(o, ng_ref[...]) * (1.0 - lam_init)).astype(BF16)


def _attn_prompt(q, kt, v, ng, lam_p, *, batch, heads, hd, seq, tq, tk, td, tr, lam_init):
    m, w = q.shape
    nq = seq // tq
    assert 2 * hd == LANES and tq % tk == 0 and tq % td == 0 and td % tr == 0
    qspec = pl.BlockSpec((tq, 2 * hd), lambda b, h, i: (b * nq + i, h))
    return pl.pallas_call(
        functools.partial(_attn_prompt_kernel, tq=tq, tk=tk, td=td, tr=tr, hd=hd, lam_init=lam_init),
        grid=(batch, heads, nq),
        in_specs=[qspec,
                  pl.BlockSpec((None, 2 * hd, seq), lambda b, h, i: (b, h, 0)),
                  pl.BlockSpec((seq, 2 * hd), lambda b, h, i: (b, h)),
                  pl.BlockSpec((1, 2 * hd), lambda b, h, i: (0, 0)),
                  pl.BlockSpec((4, hd), lambda b, h, i: (0, 0))],
        out_specs=qspec,
        out_shape=jax.ShapeDtypeStruct((m, w), BF16),
        scratch_shapes=[pltpu.VMEM((2, tq, 2 * hd), BF16), pltpu.VMEM((seq, 4 * hd), BF16),
                        pltpu.VMEM((2, tq, 2 * hd), F32), pltpu.VMEM((2, tq, 4 * hd), F32)],
        compiler_params=_cparams(("parallel", "parallel", "arbitrary")),
        name="attn_prompt",
    )(q, kt, v, ng, lam_p)


def _attn_decode_kernel(pt_ref, qbd_ref, kn_ref, vn_ref, ng_ref, lam_ref, *refs, pg, dec_seq, heads, hd, lam_init):
    del pt_ref
    k_refs = refs[:pg]
    v_refs = refs[pg:2 * pg]
    o_ref, m_scr, l_scr, acc_scr = refs[2 * pg:]
    j = pl.program_id(1)
    nj = pl.num_programs(1)
    qbd = qbd_ref[...]
    grp = 2 * dec_seq
    page = kn_ref.shape[-1]

    def pv(p, v_ref):
        outs = []
        for h in range(heads):
            v_h = v_ref[pl.ds(h, page, stride=heads), :].astype(BF16)
            outs.append(_dot(p[h * grp:(h + 1) * grp], v_h))
        return jnp.concatenate(outs, axis=0)

    @pl.when(j == 0)
    def _():
        s = _dot(qbd, kn_ref[...])
        r_i = lax.broadcasted_iota(jnp.int32, s.shape, 0) % dec_seq
        c_i = lax.broadcasted_iota(jnp.int32, s.shape, 1)
        s = jnp.where(c_i <= r_i, s, -jnp.inf)
        m0 = jnp.max(s, axis=-1, keepdims=True)
        p = jnp.exp2(s - m0)
        m_scr[...] = jnp.broadcast_to(m0, m_scr.shape)
        l_scr[...] = jnp.broadcast_to(jnp.sum(p, axis=-1, keepdims=True), l_scr.shape)
        acc_scr[...] = pv(p.astype(BF16), vn_ref)

    s = jnp.concatenate([_dot(qbd, kr[...].astype(BF16)) for kr in k_refs], axis=-1)
    m_prev = m_scr[...]
    m_new = jnp.maximum(m_prev, jnp.max(s, axis=-1, keepdims=True))
    alpha = jnp.exp2(m_prev - m_new)
    p = jnp.exp2(s - jnp.tile(m_new, (1, s.shape[1] // LANES)))
    l_scr[...] = alpha * l_scr[...] + jnp.sum(p, axis=-1, keepdims=True)
    pb = p.astype(BF16)
    upd = pv(pb[:, 0:page], v_refs[0])
    for i in range(1, pg):
        upd = upd + pv(pb[:, i * page:(i + 1) * page], v_refs[i])
    acc_scr[...] = alpha * acc_scr[...] + upd
    m_scr[...] = m_new

    @pl.when(j == nj - 1)
    def _():
        lam = _lambda(lam_ref, lam_init)
        ng = ng_ref[...]
        n = acc_scr[...] / l_scr[...]
        for h in range(heads):
            blk = n[h * grp:(h + 1) * grp]
            o = blk[0:dec_seq] - lam * blk[dec_seq:grp]
            o_ref[:, h * 2 * hd:(h + 1) * 2 * hd] = _rms_norm(o, ng) * (1.0 - lam_init)


def _attn_decode(page_table, qbd, kt_new, v_new, ng, lam_p, cache_kt, cache_v2, layer, *, pg, dec_seq, heads, hd,
                 lam_init):
    db, r, w = qbd.shape
    n_pages = page_table.shape[1]
    page = cache_kt.shape[-1]
    assert n_pages % pg == 0 and page == LANES and 2 * hd == LANES

    def kspec(i):
        return pl.BlockSpec((None, None, w, page), lambda b, j, pt, i=i: (layer, pt[b, j * pg + i], 0, 0))

    def vspec(i):
        return pl.BlockSpec((None, None, page * heads, 2 * hd),
                            lambda b, j, pt, i=i: (layer, pt[b, j * pg + i], 0, 0))

    in_specs = [pl.BlockSpec((None, r, w), lambda b, j, pt: (b, 0, 0)),
                pl.BlockSpec((None, w, page), lambda b, j, pt: (b, 0, 0)),
                pl.BlockSpec((None, page * heads, 2 * hd), lambda b, j, pt: (b, 0, 0)),
                pl.BlockSpec((1, 2 * hd), lambda b, j, pt: (0, 0)),
                pl.BlockSpec((4, hd), lambda b, j, pt: (0, 0))]
    in_specs += [kspec(i) for i in range(pg)] + [vspec(i) for i in range(pg)]
    return pl.pallas_call(
        functools.partial(_attn_decode_kernel, pg=pg, dec_seq=dec_seq, heads=heads, hd=hd, lam_init=lam_init),
        grid_spec=pltpu.PrefetchScalarGridSpec(
            num_scalar_prefetch=1,
            grid=(db, n_pages // pg),
            in_specs=in_specs,
            out_specs=pl.BlockSpec((None, dec_seq, w), lambda b, j, pt: (b, 0, 0)),
            scratch_shapes=[pltpu.VMEM((r, 2 * hd), F32)] * 3),
        out_shape=jax.ShapeDtypeStruct((db, dec_seq, w), F32),
        compiler_params=_cparams(("parallel", "arbitrary")),
        name="attn_decode",
    )(page_table, qbd, kt_new, v_new, ng, lam_p, *([cache_kt] * pg), *([cache_v2] * pg))


def _mix_kernel(x_ref, sh_ref, sc_ref, gt_ref, oa_ref, ob_ref, wga_ref, wgb_ref, wa_ref, wb_ref, wo_ref,
                g_ref, b_ref, o_ref, *, alpha):
    x = x_ref[...]
    u = (x * (1.0 + sc_ref[...]) + sh_ref[...]).astype(BF16)
    ya = _sigmoid(_dot(u, wga_ref[...])) * _dot(oa_ref[...], wa_ref[...])
    yb = _sigmoid(_dot(u, wgb_ref[...])) * _dot(ob_ref[...], wb_ref[...])
    z = _dot((ya + yb).astype(BF16), wo_ref[...])
    o_ref[...] = _layer_norm(alpha * x + gt_ref[...] * z, g_ref[...], b_ref[...])


def _mix(x, mod, layer, oa, ob, wgate, wa, wb, wo, ln_g, ln_b, *, alpha, tm, rows_per_batch):
    m, d = x.shape
    row = pl.BlockSpec((tm, d), lambda i: (i, 0))
    wsq = _resident((None, d, d), lambda i: (layer, 0, 0))
    in_specs = [row] + _mod_specs(mod, layer, 1, tm, rows_per_batch, (0, 1, 2)) + [
        row, row,
        _resident((None, d, d), lambda i: (layer, 0, 0)),
        _resident((None, d, d), lambda i: (layer, 0, 1)),
        wsq, wsq, wsq,
        pl.BlockSpec((None, None, 1, d), lambda i: (layer, 1, 0, 0)),
        pl.BlockSpec((None, None, 1, d), lambda i: (layer, 1, 0, 0)),
    ]
    return pl.pallas_call(
        functools.partial(_mix_kernel, alpha=alpha),
        grid=(m // tm,),
        in_specs=in_specs,
        out_specs=row,
        out_shape=jax.ShapeDtypeStruct((m, d), F32),
        compiler_params=_cparams(("parallel",)),
        name="mix",
    )(x, mod, mod, mod, oa, ob, wgate, wgate, wa, wb, wo, ln_g, ln_b)


def _rope_angles(pos, hd):
    half = hd // 2
    inv = ROPE_THETA ** (-jnp.arange(half, dtype=F32) * (2.0 / hd))
    ang = pos.astype(F32)[:, None] * inv[None, :]
    return jnp.cos(ang), jnp.sin(ang)


def _rope_lane_tables(cos, sin, hd):
    reps = LANES // hd
    return (jnp.tile(jnp.concatenate([cos, cos], -1), (1, reps)),
            jnp.tile(jnp.concatenate([-sin, sin], -1), (1, reps)))


def _tile(n, pref):
    t = min(n, pref)
    while n % t:
        t //= 2
    return t


MXU_WIDTH = 256


def _mxu_chunk(n):
    return MXU_WIDTH if n % MXU_WIDTH == 0 else LANES


def kernel(x_prompt, x_sample, cache_k, cache_v, state_gla, page_table, c_prompt, c_sample, w_ada, b_ada,
           ln_g, ln_b, ffn_w_in, ffn_w_out, w_in, gla_w_f2, gla_b_f2, gla_norm_g, diff_lam, diff_norm_g,
           w_branch_a, w_branch_b, w_out):
    batch, seq, d = x_prompt.shape
    db, dec_seq, _ = x_sample.shape
    depth = w_ada.shape[0]
    gla_heads, hk, hv = state_gla.shape[2:]
    n_pool, page, heads, _, hd = cache_k.shape[1:]
    dk, dv = gla_heads * hk, gla_heads * hv
    wq = 2 * heads * hd
    rank = gla_w_f2.shape[1]
    f = ffn_w_out.shape[2]
    past = page_table.shape[1] * page
    alpha = (2.0 * depth) ** 0.25
    mp, ms = batch * seq, db * dec_seq

    o_f = 2 * dk + 2 * dv
    o_dq = o_f + rank
    o_ga = o_dq + 3 * wq
    wg = w_in[:, :, 0:o_f].astype(BF16)
    rank_pad = -(-rank // LANES) * LANES
    wf1 = jnp.pad(w_in[:, :, o_f:o_dq], ((0, 0), (0, 0), (0, rank_pad - rank))).astype(BF16)
    wf2 = jnp.pad(gla_w_f2, ((0, 0), (0, rank_pad - rank), (0, 0))).astype(BF16)
    bf2 = gla_b_f2.reshape(depth, 1, dk)
    wd = w_in[:, :, o_dq:o_ga].astype(BF16)
    wkt = jnp.swapaxes(w_in[:, :, o_dq + wq:o_dq + 2 * wq], 1, 2).astype(BF16)
    wgate = w_in[:, :, o_ga:o_ga + 2 * d].astype(BF16)
    ffn_in = ffn_w_in.astype(BF16)
    ffn_out = ffn_w_out.astype(BF16)
    wa = w_branch_a.astype(BF16)
    wb = w_branch_b.astype(BF16)
    wo = w_out.astype(BF16)
    ln_g4 = ln_g.reshape(depth, 3, 1, d)
    ln_b4 = ln_b.reshape(depth, 3, 1, d)
    gla_ng = gla_norm_g.reshape(depth, 1, hv)
    diff_ng = diff_norm_g.reshape(depth, 1, 2 * hd)

    n_c = batch + db
    n_c_pad = -(-n_c // 8) * 8
    c_all = jnp.pad(jnp.concatenate([c_prompt, c_sample], 0), ((0, n_c_pad - n_c), (0, 0)))
    mod = _ada_mod(c_all, w_ada, b_ada)
    mod_p = mod[:, :batch].reshape(depth, batch, 1, N_MOD * d)
    mod_s = jnp.repeat(mod[:, batch:n_c], dec_seq, axis=1)

    cos_p, sin_p = _rope_angles(jnp.arange(seq, dtype=jnp.int32), hd)
    cos_pl, sin_pl = _rope_lane_tables(cos_p, sin_p, hd)
    cos_pt, sin_pt = cos_p.T, sin_p.T
    cos_s, sin_s = _rope_lane_tables(*_rope_angles(past + jnp.arange(dec_seq, dtype=jnp.int32), hd), hd)
    cos_s = jnp.tile(cos_s, (db, 1))
    sin_s = jnp.tile(sin_s, (db, 1))

    tm_p = _tile(seq, 512)
    fc = _mxu_chunk(f)
    tq = _tile(seq, 2048)
    tk = _tile(tq, 1024)
    td = _tile(tq, 512)
    tr = td
    gla_tile = _tile(seq, 512)
    gla_chunk = _tile(gla_tile, 64)
    dec_pad = -(-dec_seq // BF16_ROWS) * BF16_ROWS
    pg = _tile(page_table.shape[1], 16)
    cache_kt = cache_k.transpose(0, 1, 3, 4, 5, 2).reshape(depth, n_pool, wq, page)
    cache_v2 = cache_v.reshape(depth, n_pool, page * heads, 2 * hd)
    eye_hc = jnp.eye(2 * heads, dtype=BF16)

    hp = x_prompt.reshape(mp, d)
    hs = x_sample.reshape(ms, d)
    k_all = jnp.zeros((depth, batch, wq, seq), F32)
    v_all = jnp.zeros((depth, mp * heads, 2 * hd), F32)
    sp_all = jnp.zeros((depth, batch, gla_heads, hk, hv), F32)
    sd_all = jnp.zeros((depth, db, gla_heads, hk, hv), F32)
    kd, vd = [], []

    def pad_rows(t):
        t3 = t.reshape(db, dec_seq, t.shape[-1])
        return jnp.pad(t3, ((0, 0), (0, dec_pad - dec_seq), (0, 0))).reshape(db * dec_pad, t.shape[-1])

    for l in range(depth):
        lam_init = 0.8 - 0.6 * math.exp(-0.3 * l)

        hp = _ffn(hp, mod_p, l, 0, 0, ffn_in, ffn_out, ln_g4, ln_b4, alpha=alpha, tm=tm_p, fc=fc,
                  rows_per_batch=seq)
        gq, gk, gv, gr, gg = _proj_gla(hp, mod_p, l, wg, wf1, wf2, bf2, dk=dk, dv=dv, hk=hk, tm=tm_p,
                                       rows_per_batch=seq)
        k_all, v_all, dq, ktb, vb = _proj_diff_cols(hp, mod_p, l, wd, wkt, cos_pl, sin_pl, cos_pt, sin_pt,
                                                    k_all, v_all, hd=hd, heads=heads, tm=tm_p, seq=seq)
        oa, sp_all = _gla(gq, gk, gg, gv, gr, gla_ng[l], sp_all, l, rows_per_batch=seq, tile=gla_tile,
                          chunk=gla_chunk)
        ob = _attn_prompt(dq, ktb, vb, diff_ng[l], diff_lam[l], batch=batch, heads=heads, hd=hd, seq=seq,
                          tq=tq, tk=tk, td=td, tr=tr, lam_init=lam_init)
        hp = _mix(hp, mod_p, l, oa, ob, wgate, wa, wb, wo, ln_g4, ln_b4, alpha=alpha, tm=tm_p,
                  rows_per_batch=seq)
        hp = _ffn(hp, mod_p, l, 2, 1, ffn_in, ffn_out, ln_g4, ln_b4, alpha=alpha, tm=tm_p, fc=fc,
                  rows_per_batch=seq)

        hs = _ffn(hs, mod_s, l, 0, 0, ffn_in, ffn_out, ln_g4, ln_b4, alpha=alpha, tm=ms, fc=fc,
                  rows_per_batch=None)
        gq, gk, gv, gr, gg = _proj_gla(hs, mod_s, l, wg, wf1, wf2, bf2, dk=dk, dv=dv, hk=hk, tm=ms,
                                       rows_per_batch=None)
        dq, kf, kb, vf = _proj_diff_rows(hs, mod_s, l, wd, cos_s, sin_s, hd=hd, tm=ms)
        oa, sd_all = _gla(pad_rows(gq), pad_rows(gk), pad_rows(gg), pad_rows(gv), pad_rows(gr), gla_ng[l],
                          sd_all, l, state_gla, rows_per_batch=dec_pad, tile=dec_pad, chunk=dec_pad)
        oa = oa.reshape(db, dec_pad, dv)[:, :dec_seq].reshape(ms, dv)
        q5 = dq.reshape(db, dec_seq, 2 * heads, hd)
        qbd = (q5.transpose(0, 2, 1, 3)[:, :, :, None, :] * eye_hc[None, :, None, :, None]).reshape(
            db, 2 * heads * dec_seq, wq)
        kt_new = jnp.pad(kb.reshape(db, dec_seq, wq).transpose(0, 2, 1), ((0, 0), (0, 0), (0, page - dec_seq)))
        v_new = jnp.pad(vf.reshape(db, dec_seq, wq), ((0, 0), (0, page - dec_seq), (0, 0))).reshape(
            db, page * heads, 2 * hd)
        ob = _attn_decode(page_table, qbd, kt_new, v_new, diff_ng[l], diff_lam[l], cache_kt, cache_v2, l, pg=pg,
                          dec_seq=dec_seq, heads=heads, hd=hd, lam_init=lam_init)
        ob = ob.reshape(ms, wq).astype(BF16)
        hs = _mix(hs, mod_s, l, oa, ob, wgate, wa, wb, wo, ln_g4, ln_b4, alpha=alpha, tm=ms,
                  rows_per_batch=None)
        hs = _ffn(hs, mod_s, l, 2, 1, ffn_in, ffn_out, ln_g4, ln_b4, alpha=alpha, tm=ms, fc=fc,
                  rows_per_batch=None)
        kd.append(kf)
        vd.append(vf)

    k_prompt = k_all.reshape(depth, batch, heads, 2, hd, seq).transpose(0, 1, 5, 2, 3, 4)
    return (hp.reshape(batch, seq, d),
            hs.reshape(db, dec_seq, d),
            k_prompt,
            v_all.reshape(depth, batch, seq, heads, 2 * hd),
            sp_all,
            jnp.stack(kd).reshape(depth, db, dec_seq, heads, 2, hd),
            jnp.stack(vd).reshape(depth, db, dec_seq, heads, 2 * hd),
            sd_all)
```
